```python
import math
import jax, jax.numpy as jnp
from jax import lax
import numpy as np

D_MODEL = 1024
BATCH = 8
SEQ = 2048
DEPTH = 2

MIX_WIDTH = 512
N_BRANCH = 3

RG_WIDTH = MIX_WIDTH
RG_BLOCKS = 8
RG_BLOCK = RG_WIDTH // RG_BLOCKS
CONV_WIDTH = 4
RG_C = 8.0

RW_HEADS = 8
RW_HEAD = 64
RW_WIDTH = RW_HEADS * RW_HEAD
LORA_W = 64
LORA_A = 64
LORA_V = 32
LORA_G = 128
RW_GN_EPS = 64e-5
RW_COLS = 3 * RW_WIDTH + LORA_W + LORA_A + LORA_G

MLA_HEADS = 8
QK_NOPE = 64
QK_ROPE = 32
V_HEAD = 64
Q_LORA = 256
KV_LORA = 128
ROPE_THETA = 10000.0
Q_BLOCK = 128
MLA_COLS = Q_LORA + KV_LORA + QK_ROPE

RG_COLS = 2 * RG_WIDTH
GATE_COLS = N_BRANCH * D_MODEL
IN_COLS = RG_COLS + RW_COLS + MLA_COLS + GATE_COLS

N_KEYS = 128
N_EXPERTS = N_KEYS * N_KEYS
PEER_HEADS = 8
PEER_QDIM = 256
PEER_HALF = PEER_QDIM // 2
PEER_TOPK = 16
TOKEN_BLOCK = 128

ALPHA = (2.0 * DEPTH) ** 0.25
BETA = (8.0 * DEPTH) ** -0.25
LN_EPS = 1e-5

kernel_name = "hybrid_rglru_rwkv7_mla_peer_deepnorm"


def split_last(t, sizes):
    out, o = [], 0
    for s in sizes:
        out.append(t[..., o:o + s])
        o += s
    return out


def layer_norm(x, g, b):
    xf = x.astype(jnp.float32)
    mu = jnp.mean(xf, -1, keepdims=True)
    var = jnp.mean(jnp.square(xf - mu), -1, keepdims=True)
    return ((xf - mu) * lax.rsqrt(var + LN_EPS) * g.astype(jnp.float32) + b.astype(jnp.float32)).astype(x.dtype)


def rms_norm(x, g):
    xf = x.astype(jnp.float32)
    return (xf * lax.rsqrt(jnp.mean(xf * xf, -1, keepdims=True) + 1e-6) * g.astype(jnp.float32)).astype(x.dtype)


def token_shift(p):
    return jnp.pad(p, ((0, 0), (1, 0), (0, 0)))[:, :-1]


def causal_depthwise_conv(x, w, b):
    S = x.shape[1]
    xp = jnp.pad(x, ((0, 0), (CONV_WIDTH - 1, 0), (0, 0)))
    out = b
    for j in range(CONV_WIDTH):
        out = out + w[j] * xp[:, CONV_WIDTH - 1 - j: CONV_WIDTH - 1 - j + S]
    return out


def rg_lru_branch(xb, gb, conv_w, conv_b, wa, ba, wx, bx, log_a_param):
    B, S, _ = xb.shape
    xc = causal_depthwise_conv(xb, conv_w, conv_b)
    xh = xc.reshape(B, S, RG_BLOCKS, RG_BLOCK)
    r = jax.nn.sigmoid(jnp.einsum('bshi,hij->bshj', xh, wa).reshape(B, S, RG_WIDTH) + ba)
    i = jax.nn.sigmoid(jnp.einsum('bshi,hij->bshj', xh, wx).reshape(B, S, RG_WIDTH) + bx)
    log_a = -RG_C * r.astype(jnp.float32) * jax.nn.softplus(-log_a_param.astype(jnp.float32))
    a = jnp.exp(log_a)
    reset = (jnp.arange(S) == 0)[None, :, None]
    mult = jnp.where(reset, 1.0, jnp.sqrt(-jnp.expm1(2.0 * log_a)))
    b_in = mult * (i * xc).astype(jnp.float32)

    def combine(c1, c2):
        a1, h1 = c1
        a2, h2 = c2
        return a1 * a2, a2 * h1 + h2

    _, h = lax.associative_scan(combine, (a, b_in), axis=1)
    return h.astype(xb.dtype) * jax.nn.gelu(gb)


def rwkv7_branch(p, mix, w0, w2, a0, a2, g2, k_k, k_a, r_k, gn_g, gn_b, v_first, vres):
    B, S, _ = p.shape
    p = p + (token_shift(p) - p) * mix
    r, k, v, xw, xa, xg = split_last(p, (RW_WIDTH, RW_WIDTH, RW_WIDTH, LORA_W, LORA_A, LORA_G))
    w = -jax.nn.softplus(-(w0 + jnp.tanh(xw) @ w2)) - 0.5
    a = jax.nn.sigmoid(a0 + xa @ a2)
    g = jax.nn.sigmoid(xg) @ g2
    if vres is None:
        v_first = v
    else:
        v0, v1, v2 = vres
        v = v + (v_first - v) * jax.nn.sigmoid(v0 + (v @ v1) @ v2)
    hd = lambda t: t.reshape(B, S, RW_HEADS, RW_HEAD)
    kk = hd(k * k_k).astype(jnp.float32)
    kk = kk / jnp.maximum(jnp.sqrt(jnp.sum(kk * kk, -1, keepdims=True)), 1e-12)
    k = k * (1.0 + (a - 1.0) * k_a)
    decay = jnp.exp(-jnp.exp(w.astype(jnp.float32)))
    a_h = hd(a).astype(jnp.float32)

    tm = lambda t: jnp.moveaxis(t, 1, 0)
    f32h = lambda t: hd(t).astype(jnp.float32)
    xs = (tm(f32h(r)), tm(f32h(decay)), tm(f32h(k)), tm(f32h(v)), tm(-kk), tm(kk * a_h))

    def step(state, inp):
        r_t, w_t, k_t, v_t, aa_t, bb_t = inp
        sa = jnp.einsum('bhvk,bhk->bhv', state, aa_t)
        state = (state * w_t[:, :, None, :] + sa[..., None] * bb_t[:, :, None, :]
                 + v_t[..., None] * k_t[:, :, None, :])
        return state, jnp.einsum('bhvk,bhk->bhv', state, r_t)

    s0 = jnp.zeros((B, RW_HEADS, RW_HEAD, RW_HEAD), jnp.float32)
    _, ys = lax.scan(step, s0, xs)
    y = jnp.moveaxis(ys, 0, 1)
    mu = jnp.mean(y, -1, keepdims=True)
    var = jnp.mean(jnp.square(y - mu), -1, keepdims=True)
    yn = ((y - mu) * lax.rsqrt(var + RW_GN_EPS)).reshape(B, S, RW_WIDTH)
    yn = (yn * gn_g.astype(jnp.float32) + gn_b.astype(jnp.float32)).astype(p.dtype)
    bonus = jnp.sum(hd(r) * hd(k) * r_k, -1, keepdims=True) * hd(v)
    return (yn + bonus.reshape(B, S, RW_WIDTH)) * g, v_first


def apply_rope(t, cos, sin):
    half = QK_ROPE // 2
    t1, t2 = t[..., :half], t[..., half:]
    return jnp.concatenate([t1 * cos - t2 * sin, t1 * sin + t2 * cos], -1)


def mla_branch(p, q_norm, w_uq, kv_norm, w_ukv):
    B, S, _ = p.shape
    c_q, c_kv, k_rope = split_last(p, (Q_LORA, KV_LORA, QK_ROPE))
    q = (rms_norm(c_q, q_norm) @ w_uq).reshape(B, S, MLA_HEADS, QK_NOPE + QK_ROPE)
    q_nope, q_pe = q[..., :QK_NOPE], q[..., QK_NOPE:]
    kv = (rms_norm(c_kv, kv_norm) @ w_ukv).reshape(B, S, MLA_HEADS, QK_NOPE + V_HEAD)
    k_nope, v = kv[..., :QK_NOPE], kv[..., QK_NOPE:]
    pos = jnp.arange(S, dtype=jnp.float32)
    inv_freq = ROPE_THETA ** (-jnp.arange(QK_ROPE // 2, dtype=jnp.float32) / (QK_ROPE // 2))
    ang = pos[:, None] * inv_freq[None, :]
    cos, sin = jnp.cos(ang).astype(p.dtype), jnp.sin(ang).astype(p.dtype)
    q_pe = apply_rope(q_pe, cos[:, None, :], sin[:, None, :])
    k_pe = apply_rope(k_rope, cos, sin)
    scale = (QK_NOPE + QK_ROPE) ** -0.5
    nb = S // Q_BLOCK
    blk = lambda t: jnp.moveaxis(t.reshape(B, nb, Q_BLOCK, MLA_HEADS, t.shape[-1]), 1, 0)
    starts = jnp.arange(nb, dtype=jnp.int32) * Q_BLOCK
    kpos = jnp.arange(S, dtype=jnp.int32)

    def attend(args):
        qn, qp, start = args
        s = (jnp.einsum('bqhd,bkhd->bhqk', qn, k_nope)
             + jnp.einsum('bqhd,bkd->bhqk', qp, k_pe)).astype(jnp.float32) * scale
        qpos = start + jnp.arange(Q_BLOCK, dtype=jnp.int32)
        s = jnp.where(kpos[None, :] <= qpos[:, None], s, -1e30)
        pr = jax.nn.softmax(s, axis=-1).astype(v.dtype)
        return jnp.einsum('bhqk,bkhd->bqhd', pr, v)

    o = lax.map(attend, (blk(q_nope), blk(q_pe), starts))
    return jnp.moveaxis(o, 0, 1).reshape(B, S, MLA_HEADS * V_HEAD)


def peer_ffn(x, w_query, subkeys, u_table, v_table):
    B, S, D = x.shape
    T = B * S
    xt = x.reshape(T, D)
    q = (xt @ w_query).reshape(T, PEER_HEADS, 2, PEER_HALF)
    s1 = jnp.einsum('thd,nd->thn', q[:, :, 0], subkeys[0])
    s2 = jnp.einsum('thd,nd->thn', q[:, :, 1], subkeys[1])
    v1, i1 = lax.top_k(s1, PEER_TOPK)
    v2, i2 = lax.top_k(s2, PEER_TOPK)
    cand = (v1[..., :, None] + v2[..., None, :]).reshape(T, PEER_HEADS, PEER_TOPK * PEER_TOPK)
    cand_idx = (i1[..., :, None] * N_KEYS + i2[..., None, :]).reshape(T, PEER_HEADS, PEER_TOPK * PEER_TOPK)
    top, pos = lax.top_k(cand, PEER_TOPK)
    idx = jnp.take_along_axis(cand_idx, pos, axis=-1)
    gate = jax.nn.softmax(top.astype(jnp.float32), axis=-1).astype(x.dtype)
    nblk = T // TOKEN_BLOCK

    def expert_block(args):
        xb, ib, gb = args
        u = u_table[ib]
        vv = v_table[ib]
        act = jax.nn.gelu(jnp.einsum('thkd,td->thk', u, xb))
        return jnp.einsum('thk,thkd->td', gb * act, vv)

    y = lax.map(expert_block, (xt.reshape(nblk, TOKEN_BLOCK, D),
                               idx.reshape(nblk, TOKEN_BLOCK, PEER_HEADS, PEER_TOPK),
                               gate.reshape(nblk, TOKEN_BLOCK, PEER_HEADS, PEER_TOPK)))
    return y.reshape(B, S, D)


def setup_inputs(seed: int = 0) -> dict:
    key = jax.random.key(seed)
    ks = iter(jax.random.split(key, 48))
    f32 = jnp.float32
    nrm = lambda shape, scale: jax.random.normal(next(ks), shape, f32) * scale
    uni = lambda shape, lo, hi: jax.random.uniform(next(ks), shape, f32, lo, hi)
    L = DEPTH
    a_c = uni((L, RG_WIDTH), 0.9, 0.999)
    s = a_c ** (1.0 / RG_C)
    return {
        "x": nrm((BATCH, SEQ, D_MODEL), 1.0),
        "w_in": nrm((L, D_MODEL, IN_COLS), D_MODEL ** -0.5),
        "rg_conv_w": nrm((L, CONV_WIDTH, RG_WIDTH), CONV_WIDTH ** -0.5),
        "rg_conv_b": nrm((L, RG_WIDTH), 0.02),
        "rg_wa": nrm((L, RG_BLOCKS, RG_BLOCK, RG_BLOCK), RG_BLOCK ** -0.5),
        "rg_ba": nrm((L, RG_WIDTH), 0.02),
        "rg_wx": nrm((L, RG_BLOCKS, RG_BLOCK, RG_BLOCK), RG_BLOCK ** -0.5),
        "rg_bx": nrm((L, RG_WIDTH), 0.02),
        "rg_log_a": jnp.log(s) - jnp.log1p(-s),
        "rw_mix": uni((L, RW_COLS), 0.0, 1.0),
        "rw_w0": uni((L, RW_WIDTH), -6.0, -1.0),
        "rw_w2": nrm((L, LORA_W, RW_WIDTH), 0.1 * LORA_W ** -0.5),
        "rw_a0": nrm((L, RW_WIDTH), 0.1),
        "rw_a2": nrm((L, LORA_A, RW_WIDTH), 0.1 * LORA_A ** -0.5),
        "rw_g2": nrm((L, LORA_G, RW_WIDTH), LORA_G ** -0.5),
        "rw_v0": nrm((L - 1, RW_WIDTH), 0.1),
        "rw_v1": nrm((L - 1, RW_WIDTH, LORA_V), RW_WIDTH ** -0.5),
        "rw_v2": nrm((L - 1, LORA_V, RW_WIDTH), LORA_V ** -0.5),
        "rw_k_k": 0.85 + nrm((L, RW_WIDTH), 0.02),
        "rw_k_a": 1.0 + nrm((L, RW_WIDTH), 0.02),
        "rw_r_k": nrm((L, RW_HEADS, RW_HEAD), 0.1),
        "rw_gn_g": 1.0 + nrm((L, RW_WIDTH), 0.02),
        "rw_gn_b": nrm((L, RW_WIDTH), 0.02),
        "mla_q_norm": 1.0 + nrm((L, Q_LORA), 0.02),
        "mla_w_uq": nrm((L, Q_LORA, MLA_HEADS * (QK_NOPE + QK_ROPE)), Q_LORA ** -0.5),
        "mla_kv_norm": 1.0 + nrm((L, KV_LORA), 0.02),
        "mla_w_ukv": nrm((L, KV_LORA, MLA_HEADS * (QK_NOPE + V_HEAD)), KV_LORA ** -0.5),
        "w_branch": nrm((L, N_BRANCH, MIX_WIDTH, D_MODEL), BETA * MIX_WIDTH ** -0.5),
        "w_out": nrm((L, D_MODEL, D_MODEL), BETA * D_MODEL ** -0.5),
        "ln1_g": 1.0 + nrm((L, D_MODEL), 0.02),
        "ln1_b": nrm((L, D_MODEL), 0.02),
        "peer_w_query": nrm((L, D_MODEL, PEER_HEADS * PEER_QDIM), D_MODEL ** -0.5),
        "peer_subkeys": nrm((L, 2, N_KEYS, PEER_HALF), PEER_HALF ** -0.5),
        "peer_u": nrm((L, N_EXPERTS, D_MODEL), D_MODEL ** -0.5),
        "peer_v": nrm((L, N_EXPERTS, D_MODEL), BETA * PEER_HEADS ** -0.5),
        "ln2_g": 1.0 + nrm((L, D_MODEL), 0.02),
        "ln2_b": nrm((L, D_MODEL), 0.02),
    }


def reference(x, w_in, rg_conv_w, rg_conv_b, rg_wa, rg_ba, rg_wx, rg_bx, rg_log_a,
              rw_mix, rw_w0, rw_w2, rw_a0, rw_a2, rw_g2, rw_v0, rw_v1, rw_v2,
              rw_k_k, rw_k_a, rw_r_k, rw_gn_g, rw_gn_b,
              mla_q_norm, mla_w_uq, mla_kv_norm, mla_w_ukv,
              w_branch, w_out, ln1_g, ln1_b,
              peer_w_query, peer_subkeys, peer_u, peer_v, ln2_g, ln2_b):
    B, S, D = x.shape
    v_first = None
    for l in range(DEPTH):
        p = x @ w_in[l]
        pa_x, pa_g, pb, pc, pg = split_last(p, (RG_WIDTH, RG_WIDTH, RW_COLS, MLA_COLS, GATE_COLS))
        y_a = rg_lru_branch(pa_x, pa_g, rg_conv_w[l], rg_conv_b[l], rg_wa[l], rg_ba[l],
                            rg_wx[l], rg_bx[l], rg_log_a[l])
        vres = None if l == 0 else (rw_v0[l - 1], rw_v1[l - 1], rw_v2[l - 1])
        y_b, v_first = rwkv7_branch(pb, rw_mix[l], rw_w0[l], rw_w2[l], rw_a0[l], rw_a2[l],
                                    rw_g2[l], rw_k_k[l], rw_k_a[l], rw_r_k[l],
                                    rw_gn_g[l], rw_gn_b[l], v_first, vres)
        y_c = mla_branch(pc, mla_q_norm[l], mla_w_uq[l], mla_kv_norm[l], mla_w_ukv[l])
        y_cat = jnp.stack([y_a, y_b, y_c], axis=2)
        gates = jax.nn.sigmoid(pg).reshape(B, S, N_BRANCH, D)
        merged = jnp.sum(jnp.einsum('bsnc,ncd->bsnd', y_cat, w_branch[l]) * gates, axis=2)
        x = layer_norm(ALPHA * x + merged @ w_out[l], ln1_g[l], ln1_b[l])
        y = peer_ffn(x, peer_w_query[l], peer_subkeys[l], peer_u[l], peer_v[l])
        x = layer_norm(ALPHA * x + y, ln2_g[l], ln2_b[l])
    return x
```

```python
import functools
import math

import jax
import jax.numpy as jnp
from jax import lax
from jax.experimental import pallas as pl
from jax.experimental.pallas import tpu as pltpu

F32 = jnp.float32
BF16 = jnp.bfloat16

N_BRANCH = 3
MIX_WIDTH = 512
RG_WIDTH = 512
RG_BLOCK = 64
CONV_WIDTH = 4
RG_C = 8.0
RW_HEADS = 8
RW_HEAD = 64
RW_WIDTH = 512
LORA_W = 64
LORA_A = 64
LORA_V = 32
LORA_G = 128
RW_GN_EPS = 64e-5
RW_COLS = 3 * RW_WIDTH + LORA_W + LORA_A + LORA_G
MLA_HEADS = 8
QK_NOPE = 64
QK_ROPE = 32
V_HEAD = 64
Q_LORA = 256
KV_LORA = 128
ROPE_THETA = 10000.0
MLA_COLS = Q_LORA + KV_LORA + QK_ROPE
N_KEYS = 128
PEER_HEADS = 8
PEER_HALF = 128
PEER_TOPK = 16
LN_EPS = 1e-5

LANES = 128
V7X_VMEM_BYTES = 64 * 1024 * 1024
MIB = 1024 * 1024

MM_TM = 1024
RW_TS = 256
RW_CHUNK = 64
MLA_TM = 512
ATT_TQ = 256
MERGE_TM = 512
PEERA_TM = 256
PEERB_TM = 512
PEERB_EC = 1024


def _cp(sem, vmem_mib):
    return pltpu.CompilerParams(dimension_semantics=sem, vmem_limit_bytes=vmem_mib * MIB)


def _bf(x):
    return x.astype(BF16)


def _dot(a, b):
    return jnp.dot(a, b, preferred_element_type=F32)


def _dot_nt(a, b):
    return lax.dot_general(a, b, (((1,), (1,)), ((), ())), preferred_element_type=F32)


def _split(x):
    hi = x.astype(BF16)
    lo = (x - hi.astype(F32)).astype(BF16)
    return hi, lo


def _dot_hilo(x, w_bf):
    hi, lo = _split(x)
    return _dot(hi, w_bf) + _dot(lo, w_bf)


def _dot3(a, b):
    ah, al = _split(a)
    bh, bl = _split(b)
    return _dot(ah, bh) + (_dot(ah, bl) + _dot(al, bh))


def _shift_rows(x, k, fill, rows):
    return jnp.where(rows >= k, pltpu.roll(x, k, 0), fill)


def _log1p(u):
    w = 1.0 + u
    d = w - 1.0
    return jnp.where(d == 0.0, u, jnp.log(w) * (u / jnp.where(d == 0.0, 1.0, d)))


def _softplus(z):
    return jnp.maximum(z, 0.0) + _log1p(jnp.exp(-jnp.abs(z)))


def _neg_expm1(x):
    u = jnp.exp(x)
    lu = jnp.log(u)
    safe = jnp.where(lu == 0.0, 1.0, lu)
    r = (1.0 - u) * x / safe
    r = jnp.where(u == 1.0, -x, r)
    return jnp.where(u == 0.0, 1.0, r)


def _layer_norm(z, g, b):
    mu = jnp.mean(z, axis=-1, keepdims=True)
    d = z - mu
    var = jnp.mean(d * d, axis=-1, keepdims=True)
    return d * lax.rsqrt(var + LN_EPS) * g + b


def _mm_kernel(x_ref, w_ref, o_ref):
    o_ref[...] = _dot(_bf(x_ref[...]), w_ref[...]).astype(o_ref.dtype)


def _matmul(x, w, tn, name, out_dtype=F32):
    m, k = x.shape
    n = w.shape[1]
    tm = min(MM_TM, m)
    return pl.pallas_call(
        _mm_kernel,
        grid=(m // tm, n // tn),
        in_specs=[pl.BlockSpec((tm, k), lambda i, j: (i, 0)),
                  pl.BlockSpec((k, tn), lambda i, j: (0, j))],
        out_specs=pl.BlockSpec((tm, tn), lambda i, j: (i, j)),
        out_shape=jax.ShapeDtypeStruct((m, n), out_dtype),
        name=name,
        compiler_params=_cp(("parallel", "parallel"), 40),
    )(x, w)


def _rg_kernel(xb_ref, gb_ref, cw_ref, cb_ref, wa_ref, ba_ref, wx_ref, bx_ref, la_ref, o_ref):
    x = xb_ref[0]
    s = x.shape[0]
    rows = lax.broadcasted_iota(jnp.int32, x.shape, 0)
    cw = cw_ref[...]
    xc = cb_ref[...] + cw[0:1, :] * x
    for j in range(1, CONV_WIDTH):
        xc = xc + cw[j:j + 1, :] * _shift_rows(x, j, 0.0, rows)
    xcb = _bf(xc)
    r = jax.nn.sigmoid(_dot(xcb, wa_ref[0]) + ba_ref[...])
    i = jax.nn.sigmoid(_dot(xcb, wx_ref[0]) + bx_ref[...])
    log_a = (-RG_C) * r * _softplus(-la_ref[...])
    a = jnp.exp(log_a)
    mult = jnp.where(rows == 0, 1.0, jnp.sqrt(_neg_expm1(2.0 * log_a)))
    b = mult * (i * xc)
    k = 1
    while k < s:
        a_s = _shift_rows(a, k, 1.0, rows)
        b_s = _shift_rows(b, k, 0.0, rows)
        b = a * b_s + b
        a = a * a_s
        k *= 2
    o_ref[0] = b * jax.nn.gelu(gb_ref[0])


def _rg_branch(p_rg, cw, cb, wa_bd, ba, wx_bd, bx, la):
    bsz, s, _ = p_rg.shape
    nct = RG_WIDTH // LANES
    vec = lambda: pl.BlockSpec((1, LANES), lambda b, c: (0, c))
    return pl.pallas_call(
        _rg_kernel,
        grid=(bsz, nct),
        in_specs=[pl.BlockSpec((1, s, LANES), lambda b, c: (b, 0, c)),
                  pl.BlockSpec((1, s, LANES), lambda b, c: (b, 0, c + nct)),
                  pl.BlockSpec((CONV_WIDTH, LANES), lambda b, c: (0, c)),
                  vec(),
                  pl.BlockSpec((1, LANES, LANES), lambda b, c: (c, 0, 0)),
                  vec(),
                  pl.BlockSpec((1, LANES, LANES), lambda b, c: (c, 0, 0)),
                  vec(), vec()],
        out_specs=pl.BlockSpec((1, s, LANES), lambda b, c: (b, 0, c)),
        out_shape=jax.ShapeDtypeStruct((bsz, s, RG_WIDTH), F32),
        name="rg_lru",
        compiler_params=_cp(("parallel", "parallel"), 48),
    )(p_rg, p_rg, cw, cb, wa_bd, ba, wx_bd, bx, la)


def _head_ones(n):
    sh = jnp.int32(RW_HEAD.bit_length() - 1)
    ri = lax.shift_right_logical(lax.broadcasted_iota(jnp.int32, (n, n), 0), sh)
    ci = lax.shift_right_logical(lax.broadcasted_iota(jnp.int32, (n, n), 1), sh)
    return jnp.where(ri == ci, 1.0, 0.0).astype(BF16)


def _rw_prep_kernel(has_vres, pb_ref, prev_ref, mix_ref, w0_ref, w2_ref, a0_ref, a2_ref, g2_ref,
                    kk_ref, ka_ref, *rest):
    if has_vres:
        vf_ref, v0_ref, v1_ref, v2_ref = rest[:4]
        rest = rest[4:]
    r_o, lw_o, k_o, v_o, aa_o, bb_o, g_o = rest
    t = pl.program_id(1)
    p = pb_ref[0]
    rows = lax.broadcasted_iota(jnp.int32, p.shape, 0)
    prev = jnp.where(t == 0, 0.0, prev_ref[0][7:8, :])
    shifted = jnp.where(rows == 0, prev, pltpu.roll(p, 1, 0))
    p = p + (shifted - p) * mix_ref[...]
    w = RW_WIDTH
    r = p[:, 0:w]
    k = p[:, w:2 * w]
    v = p[:, 2 * w:3 * w]
    xwa = p[:, 3 * w:3 * w + LORA_W + LORA_A]
    xg = p[:, 3 * w + LORA_W + LORA_A:]
    wl = -_softplus(-(w0_ref[...] + _dot(_bf(jnp.tanh(xwa)), w2_ref[...]))) - 0.5
    a = jax.nn.sigmoid(a0_ref[...] + _dot(_bf(xwa), a2_ref[...]))
    g = _dot(_bf(jax.nn.sigmoid(xg)), g2_ref[...])
    if has_vres:
        lo = _dot(_bf(v), v1_ref[...])
        v = v + (vf_ref[0] - v) * jax.nn.sigmoid(v0_ref[...] + _dot(_bf(lo), v2_ref[...]))
    kk = k * kk_ref[...]
    ss = _dot_hilo(kk * kk, _head_ones(w))
    kk = kk / jnp.maximum(jnp.sqrt(ss), 1e-12)
    r_o[0] = r
    lw_o[0] = -jnp.exp(wl)
    k_o[0] = k * (1.0 + (a - 1.0) * ka_ref[...])
    v_o[0] = v
    aa_o[0] = -kk
    bb_o[0] = kk * a
    g_o[0] = g


def _rw_prep(pb, mix, w0, w2p, a0, a2p, g2, k_k, k_a, vres):
    bsz, s, cols = pb.shape
    ts = min(RW_TS, s)
    w = RW_WIDTH
    full = lambda shape: pl.BlockSpec(shape, lambda b, t: (0,) * len(shape))
    tok = lambda c: pl.BlockSpec((1, ts, c), lambda b, t: (b, t, 0))
    in_specs = [tok(cols),
                pl.BlockSpec((1, 8, cols), lambda b, t: (b, jnp.maximum(t * (ts // 8) - 1, 0), 0)),
                full((1, cols)), full((1, w)), full((LANES, w)), full((1, w)), full((LANES, w)),
                full((LORA_G, w)), full((1, w)), full((1, w))]
    args = [pb, pb, mix, w0, w2p, a0, a2p, g2, k_k, k_a]
    if vres is not None:
        v_first, v0, v1p, v2p = vres
        in_specs += [tok(w), full((1, w)), full((w, LANES)), full((LANES, w))]
        args += [v_first, v0, v1p, v2p]
    out = jax.ShapeDtypeStruct((bsz, s, w), F32)
    return pl.pallas_call(
        functools.partial(_rw_prep_kernel, vres is not None),
        grid=(bsz, s // ts),
        in_specs=in_specs,
        out_specs=[tok(w)] * 7,
        out_shape=[out] * 7,
        name="rwkv_prep",
        compiler_params=_cp(("parallel", "parallel"), 40),
    )(*args)


def _rw_scan_kernel(r_ref, lw_ref, k_ref, v_ref, aa_ref, bb_ref, g_ref, rk_ref, gng_ref, gnb_ref,
                    o_ref, st_ref):
    c = pl.program_id(1)

    @pl.when(c == 0)
    def _():
        st_ref[...] = jnp.zeros_like(st_ref)

    ch = r_ref.shape[1]
    c2 = 2 * ch
    rows = lax.broadcasted_iota(jnp.int32, (ch, LANES), 0)
    head0 = lax.broadcasted_iota(jnp.int32, (ch, LANES), 1) < RW_HEAD
    ri = lax.broadcasted_iota(jnp.int32, (c2, c2), 0)
    ci = lax.broadcasted_iota(jnp.int32, (c2, c2), 1)
    blk_lo = jnp.where(ri >= ch, ch, 0)
    in_blk = jnp.where(ci >= blk_lo, jnp.where(ci < blk_lo + ch, 1.0, 0.0), 0.0)
    strict = jnp.where(ci < ri, in_blk, 0.0)
    incl = jnp.where(ci <= ri, in_blk, 0.0)
    eye2 = jnp.where(ci == ri, 1.0, 0.0)
    li = lax.broadcasted_iota(jnp.int32, (LANES, LANES), 0)
    lj = lax.broadcasted_iota(jnp.int32, (LANES, LANES), 1)
    eye_l = li == lj
    ones_hd = _head_ones(LANES)

    def stack(x):
        return jnp.concatenate([jnp.where(head0, x, 0.0), jnp.where(head0, 0.0, x)], axis=0)

    for p in range(RW_WIDTH // LANES):
        sl = slice(p * LANES, (p + 1) * LANES)
        lw = lw_ref[0][:, sl]
        cum = lw
        kk = 1
        while kk < ch:
            cum = cum + _shift_rows(cum, kk, 0.0, rows)
            kk *= 2
        tot = cum[ch - 1:ch, :]
        e_in = jnp.exp(cum)
        e_out = jnp.exp(-cum)
        e_ex = jnp.exp(cum - lw)
        e_end = jnp.exp(tot - cum)
        r = r_ref[0][:, sl]
        k = k_ref[0][:, sl]
        v = v_ref[0][:, sl]
        aa = aa_ref[0][:, sl]
        bb = bb_ref[0][:, sl]
        a_s = stack(aa * e_ex)
        r_s = stack(r * e_in)
        b_s = stack(bb * e_out)
        k_s = stack(k * e_out)
        v_s = stack(v)
        gram = _dot_nt(_bf(jnp.concatenate([a_s, r_s], axis=0)),
                       _bf(jnp.concatenate([b_s, k_s], axis=0)))
        l_ab = gram[:c2, :c2] * strict
        l_ak = gram[:c2, c2:] * strict
        m_rb = gram[c2:, :c2] * incl
        m_rk = gram[c2:, c2:] * incl
        pw = l_ab
        tinv = eye2 + l_ab
        n = 2
        while n < ch:
            pwb = _bf(pw)
            pw = _dot(pwb, pwb)
            tinv = tinv + _dot(_bf(tinv), _bf(pw))
            n *= 2
        v_sb = _bf(v_s)
        lakv = _dot(_bf(l_ak), v_sb)
        wmat = _dot(_bf(tinv), _bf(jnp.concatenate([a_s, lakv], axis=1)))
        wb = _bf(wmat)
        qy = _dot(_bf(m_rb), wb)
        q_hat = r_s + qy[:, :LANES]
        y_loc = qy[:, LANES:] + _dot(_bf(m_rk), v_sb)
        bh_t = _bf(stack(bb * e_end).T)
        kh_t = _bf(stack(k * e_end).T)
        ac = _dot(bh_t, wb)
        a_hat = jnp.where(eye_l, jnp.exp(tot), 0.0) + ac[:, :LANES]
        c_hat = ac[:, LANES:] + _dot(kh_t, v_sb)
        st = st_ref[p]
        y_s = _dot3(q_hat, st) + y_loc
        st_ref[p] = _dot3(a_hat, st) + c_hat
        y = y_s[:ch] + y_s[ch:]
        mu = _dot_hilo(y, ones_hd) * (1.0 / RW_HEAD)
        d = y - mu
        var = _dot_hilo(d * d, ones_hd) * (1.0 / RW_HEAD)
        yn = d * lax.rsqrt(var + RW_GN_EPS) * gng_ref[:, sl] + gnb_ref[:, sl]
        bonus = _dot_hilo(r * k * rk_ref[:, sl], ones_hd) * v
        o_ref[0, :, sl] = (yn + bonus) * g_ref[0][:, sl]


def _rw_scan(r, lw, k, v, aa, bb, g, r_k, gn_g, gn_b):
    bsz, s, w = r.shape
    ch = min(RW_CHUNK, s)
    tok = pl.BlockSpec((1, ch, w), lambda b, c: (b, c, 0))
    vec = pl.BlockSpec((1, w), lambda b, c: (0, 0))
    return pl.pallas_call(
        _rw_scan_kernel,
        grid=(bsz, s // ch),
        in_specs=[tok] * 7 + [vec] * 3,
        out_specs=tok,
        out_shape=jax.ShapeDtypeStruct((bsz, s, w), F32),
        scratch_shapes=[pltpu.VMEM((w // LANES, LANES, LANES), F32)],
        name="rwkv_scan",
        compiler_params=_cp(("parallel", "arbitrary"), 40),
    )(r, lw, k, v, aa, bb, g, r_k, gn_g, gn_b)


def _rms(x, g):
    return x * lax.rsqrt(jnp.mean(x * x, axis=-1, keepdims=True) + 1e-6) * g


def _mla_proj_kernel(pm_ref, qn_ref, wuq_ref, kvn_ref, wukv_ref, cos_ref, sin_ref, q_o, k_o, v_o):
    c = pm_ref[...]
    hn = MLA_HEADS * QK_NOPE
    q = _dot(_bf(_rms(c[:, :Q_LORA], qn_ref[...])), wuq_ref[...])
    kv = _dot(_bf(_rms(c[:, Q_LORA:Q_LORA + KV_LORA], kvn_ref[...])), wukv_ref[...])
    k1 = c[:, Q_LORA + KV_LORA:Q_LORA + KV_LORA + LANES]
    k2 = c[:, Q_LORA + KV_LORA + LANES:]
    cos = cos_ref[...]
    sin = sin_ref[...]
    q1 = q[:, hn:hn + LANES]
    q2 = q[:, hn + LANES:]
    q_o[:, :hn] = _bf(q[:, :hn])
    q_o[:, hn:hn + LANES] = _bf(q1 * cos - q2 * sin)
    q_o[:, hn + LANES:] = _bf(q1 * sin + q2 * cos)
    k_o[:, :hn] = _bf(kv[:, :hn])
    k_o[:, hn:hn + LANES] = _bf(k1 * cos - k2 * sin)
    k_o[:, hn + LANES:] = _bf(k1 * sin + k2 * cos)
    v_o[...] = _bf(kv[:, hn:])


def _mla_proj(pm, q_norm, wuq, kv_norm, wukv, cos_t, sin_t, s):
    t, cols = pm.shape
    tm = min(MLA_TM, s)
    nst = s // tm
    hn = MLA_HEADS * QK_NOPE
    qw = hn + 2 * LANES
    full = lambda shape: pl.BlockSpec(shape, lambda i: (0,) * len(shape))
    tok = lambda c: pl.BlockSpec((tm, c), lambda i: (i, 0))
    pos = pl.BlockSpec((tm, LANES), lambda i: (i % nst, 0))
    return pl.pallas_call(
        _mla_proj_kernel,
        grid=(t // tm,),
        in_specs=[tok(cols), full((1, Q_LORA)), full((Q_LORA, qw)), full((1, KV_LORA)),
                  full((KV_LORA, 2 * hn)), pos, pos],
        out_specs=[tok(qw), tok(qw), tok(hn)],
        out_shape=[jax.ShapeDtypeStruct((t, qw), BF16), jax.ShapeDtypeStruct((t, qw), BF16),
                   jax.ShapeDtypeStruct((t, hn), BF16)],
        name="mla_proj",
        compiler_params=_cp(("parallel",), 32),
    )(pm, q_norm, wuq, kv_norm, wukv, cos_t, sin_t)


def _attn_kernel(q_ref, k_ref, v_ref, o_ref, *, scale):
    i = pl.program_id(2)
    q = q_ref[0, 0]
    k = k_ref[0, 0]
    tq = q.shape[0]
    s = _dot_nt(q, k) * scale
    qpos = i * tq + lax.broadcasted_iota(jnp.int32, s.shape, 0)
    kpos = lax.broadcasted_iota(jnp.int32, s.shape, 1)
    s = jnp.where(kpos <= qpos, s, -1e30)
    m = jnp.max(s, axis=-1, keepdims=True)
    p = jnp.exp(s - m)
    l = jnp.sum(p, axis=-1, keepdims=True)
    o = _dot(_bf(p), v_ref[0, 0]) / l
    o_ref[0, 0] = o.astype(o_ref.dtype)


def _attention(q, k, v):
    bsz, h, s, dq = q.shape
    dv = v.shape[-1]
    tq = min(ATT_TQ, s)
    return pl.pallas_call(
        functools.partial(_attn_kernel, scale=(QK_NOPE + QK_ROPE) ** -0.5),
        grid=(bsz, h, s // tq),
        in_specs=[pl.BlockSpec((1, 1, tq, dq), lambda b, hh, i: (b, hh, i, 0)),
                  pl.BlockSpec((1, 1, s, dq), lambda b, hh, i: (b, hh, 0, 0)),
                  pl.BlockSpec((1, 1, s, dv), lambda b, hh, i: (b, hh, 0, 0))],
        out_specs=pl.BlockSpec((1, 1, tq, dv), lambda b, hh, i: (b, hh, i, 0)),
        out_shape=jax.ShapeDtypeStruct((bsz, h, s, dv), BF16),
        name="mla_attention",
        compiler_params=_cp(("parallel", "parallel", "parallel"), 40),
    )(q, k, v)


def _merge_kernel(alpha, ya_ref, yb_ref, yc_ref, pg_ref, x_ref, wb_ref, wo_ref, g_ref, b_ref,
                  o_ref, ob_ref):
    d = x_ref.shape[1]
    merged = jax.nn.sigmoid(pg_ref[:, 0:d]) * _dot(_bf(ya_ref[...]), wb_ref[0])
    merged = merged + jax.nn.sigmoid(pg_ref[:, d:2 * d]) * _dot(_bf(yb_ref[...]), wb_ref[1])
    merged = merged + jax.nn.sigmoid(pg_ref[:, 2 * d:3 * d]) * _dot(_bf(yc_ref[...]), wb_ref[2])
    z = alpha * x_ref[...] + _dot(_bf(merged), wo_ref[...])
    out = _layer_norm(z, g_ref[...], b_ref[...])
    o_ref[...] = out
    ob_ref[...] = _bf(out)


def _merge(alpha, ya, yb, yc, pg, x, w_branch, w_out, g, b):
    t, d = x.shape
    tm = min(MERGE_TM, t)
    tok = lambda c: pl.BlockSpec((tm, c), lambda i: (i, 0))
    full = lambda shape: pl.BlockSpec(shape, lambda i: (0,) * len(shape))
    return pl.pallas_call(
        functools.partial(_merge_kernel, alpha),
        grid=(t // tm,),
        in_specs=[tok(MIX_WIDTH), tok(MIX_WIDTH), tok(MIX_WIDTH), tok(N_BRANCH * d), tok(d),
                  full((N_BRANCH, MIX_WIDTH, d)), full((d, d)), full((1, d)), full((1, d))],
        out_specs=[tok(d), tok(d)],
        out_shape=[jax.ShapeDtypeStruct((t, d), F32), jax.ShapeDtypeStruct((t, d), BF16)],
        name="merge_ln",
        compiler_params=_cp(("parallel",), 48),
    )(ya, yb, yc, pg, x, w_branch, w_out, g, b)


def _top16(x, kidx):
    rank = jnp.full(x.shape, float(PEER_TOPK), F32)
    vals = []
    nk = float(x.shape[0])
    for b in range(PEER_TOPK):
        m = jnp.max(x, axis=0, keepdims=True)
        first = jnp.min(jnp.where(x == m, kidx, nk), axis=0, keepdims=True)
        hit = kidx == first
        rank = jnp.where(hit, float(b), rank)
        x = jnp.where(hit, -jnp.inf, x)
        vals.append(m)
    return jnp.concatenate(vals, axis=0), rank


def _peer_a_kernel(x_ref, wq_ref, k1_ref, k2_ref, rk2_o, e2_o, n_o, eps_o, q_scr):
    q_scr[...] = _bf(_dot(x_ref[...], wq_ref[...]))
    tm = x_ref.shape[0]
    kidx = lax.broadcasted_iota(jnp.int32, (N_KEYS, tm), 0).astype(F32)
    aidx = lax.broadcasted_iota(jnp.int32, (PEER_TOPK, tm), 0).astype(F32)

    def head(h, carry):
        base = pl.multiple_of(h * (2 * PEER_HALF), 2 * PEER_HALF)
        s1 = _dot_nt(k1_ref[...], q_scr[:, pl.ds(base, PEER_HALF)])
        s2 = _dot_nt(k2_ref[...], q_scr[:, pl.ds(base + PEER_HALF, PEER_HALF)])
        v1, rk1 = _top16(s1, kidx)
        v2, rk2 = _top16(s2, kidx)
        cnt = jnp.zeros((PEER_TOPK, tm), F32)
        front = v1 + v2[0:1, :]
        for _ in range(PEER_TOPK):
            m = jnp.max(front, axis=0, keepdims=True)
            a_star = jnp.min(jnp.where(front == m, aidx, float(PEER_TOPK)), axis=0, keepdims=True)
            hit = aidx == a_star
            cnt = cnt + jnp.where(hit, 1.0, 0.0)
            nxt = jnp.sum(jnp.where(hit, cnt, 0.0), axis=0, keepdims=True)
            v2n = jnp.max(jnp.where(aidx == nxt, v2, -jnp.inf), axis=0, keepdims=True)
            front = jnp.where(hit, v1 + v2n, front)
        e1 = jnp.exp(v1 - v1[0:1, :])
        e2 = jnp.exp(v2 - v2[0:1, :])
        pref = jnp.zeros((PEER_TOPK, tm), F32)
        for b in range(PEER_TOPK):
            pref = pref + jnp.where(cnt > float(b), e2[b:b + 1, :], 0.0)
        z = jnp.sum(e1 * pref, axis=0, keepdims=True)
        ncount = jnp.zeros((N_KEYS, tm), F32)
        for a in range(PEER_TOPK):
            ncount = jnp.where(rk1 == float(a), cnt[a:a + 1, :], ncount)
        rk2_o[h] = rk2
        e2_o[h] = jnp.exp(s2 - v2[0:1, :])
        n_o[h] = ncount
        eps_o[h] = jnp.exp(s1 - v1[0:1, :]) / z
        return carry

    lax.fori_loop(0, PEER_HEADS, head, 0)


def _peer_a(xb, wq, k1, k2):
    t, d = xb.shape
    tm = min(PEERA_TM, t)
    qd = wq.shape[1]
    full = lambda shape: pl.BlockSpec(shape, lambda i: (0,) * len(shape))
    tab = pl.BlockSpec((PEER_HEADS, N_KEYS, tm), lambda i: (0, 0, i))
    out = jax.ShapeDtypeStruct((PEER_HEADS, N_KEYS, t), F32)
    return pl.pallas_call(
        _peer_a_kernel,
        grid=(t // tm,),
        in_specs=[pl.BlockSpec((tm, d), lambda i: (i, 0)), full((d, qd)),
                  full((N_KEYS, PEER_HALF)), full((N_KEYS, PEER_HALF))],
        out_specs=[tab] * 4,
        out_shape=[out] * 4,
        scratch_shapes=[pltpu.VMEM((tm, qd), BF16)],
        name="peer_topk",
        compiler_params=_cp(("parallel",), 40),
    )(xb, wq, k1, k2)


def _peer_b_kernel(alpha, xb_ref, u_ref, vt_ref, rk2_ref, e2_ref, n_ref, eps_ref, x_ref, g_ref, b_ref,
                   o_ref, acc_ref, act_ref, h_ref):
    j = pl.program_id(1)
    nblk = u_ref.shape[0] // N_KEYS

    @pl.when(j == 0)
    def _():
        acc_ref[...] = jnp.zeros_like(acc_ref)

    act_ref[...] = _dot_nt(u_ref[...], xb_ref[...])

    def blk(il, carry):
        i = j * nblk + il
        row0 = pl.multiple_of(il * N_KEYS, N_KEYS)
        gate = None
        for h in range(PEER_HEADS):
            cnt = n_ref[h, pl.ds(i, 1), :]
            eps = eps_ref[h, pl.ds(i, 1), :]
            term = jnp.where(rk2_ref[h] < cnt, e2_ref[h] * eps, 0.0)
            gate = term if gate is None else gate + term
        act = act_ref[pl.ds(row0, N_KEYS), :]
        h_ref[pl.ds(row0, N_KEYS), :] = _bf(jax.nn.gelu(act) * gate)
        return carry

    lax.fori_loop(0, nblk, blk, 0)
    acc_ref[...] += _dot(vt_ref[...], h_ref[...])

    @pl.when(j == pl.num_programs(1) - 1)
    def _():
        z = alpha * x_ref[...] + acc_ref[...].T
        o_ref[...] = _layer_norm(z, g_ref[...], b_ref[...])


def _peer_b(alpha, xb, u_bf, vt_bf, rk2, e2, ncnt, eps, x, g, b):
    t, d = x.shape
    ne = u_bf.shape[0]
    tm = min(PEERB_TM, t)
    ec = min(PEERB_EC, ne)
    tab = pl.BlockSpec((PEER_HEADS, N_KEYS, tm), lambda i, j: (0, 0, i))
    vec = pl.BlockSpec((1, d), lambda i, j: (0, 0))
    return pl.pallas_call(
        functools.partial(_peer_b_kernel, alpha),
        grid=(t // tm, ne // ec),
        in_specs=[pl.BlockSpec((tm, d), lambda i, j: (i, 0)),
                  pl.BlockSpec((ec, d), lambda i, j: (j, 0)),
                  pl.BlockSpec((d, ec), lambda i, j: (0, j)),
                  tab, tab, tab, tab,
                  pl.BlockSpec((tm, d), lambda i, j: (i, 0)), vec, vec],
        out_specs=pl.BlockSpec((tm, d), lambda i, j: (i, 0)),
        out_shape=jax.ShapeDtypeStruct((t, d), F32),
        scratch_shapes=[pltpu.VMEM((d, tm), F32), pltpu.VMEM((ec, tm), F32), pltpu.VMEM((ec, tm), BF16)],
        name="peer_dense",
        compiler_params=_cp(("parallel", "arbitrary"), 52),
    )(xb, u_bf, vt_bf, rk2, e2, ncnt, eps, x, g, b)


def _pair_block_diag(w):
    nb = w.shape[0]
    w4 = w.reshape(nb // 2, 2, RG_BLOCK, RG_BLOCK)
    z = jnp.zeros_like(w4[:, 0])
    top = jnp.concatenate([w4[:, 0], z], axis=-1)
    bot = jnp.concatenate([z, w4[:, 1]], axis=-1)
    return _bf(jnp.concatenate([top, bot], axis=-2))


def _row(v):
    return v.reshape(1, -1)


def kernel(x, w_in, rg_conv_w, rg_conv_b, rg_wa, rg_ba, rg_wx, rg_bx, rg_log_a, rw_mix, rw_w0, rw_w2, rw_a0, rw_a2, rw_g2, rw_v0, rw_v1, rw_v2, rw_k_k, rw_k_a, rw_r_k, rw_gn_g, rw_gn_b, mla_q_norm, mla_w_uq, mla_kv_norm, mla_w_ukv, w_branch, w_out, ln1_g, ln1_b, peer_w_query, peer_subkeys, peer_u, peer_v, ln2_g, ln2_b):
    bsz, s, d = x.shape
    depth = w_in.shape[0]
    t = bsz * s
    alpha = (2.0 * depth) ** 0.25
    hn = MLA_HEADS * QK_NOPE
    half = QK_ROPE // 2

    pos = jnp.arange(s, dtype=F32)
    inv_freq = ROPE_THETA ** (-jnp.arange(half, dtype=F32) / half)
    ang = pos[:, None] * inv_freq[None, :]
    cos_t = jnp.tile(jnp.cos(ang).astype(F32), (1, MLA_HEADS))
    sin_t = jnp.tile(jnp.sin(ang).astype(F32), (1, MLA_HEADS))

    xt = x.reshape(t, d)
    v_first = None
    for l in range(depth):
        wl = w_in[l]
        o_rw = 2 * RG_WIDTH
        o_mla = o_rw + RW_COLS
        o_gate = o_mla + MLA_COLS
        o_rope = o_mla + Q_LORA + KV_LORA
        w_rg = _bf(wl[:, :o_rw])
        w_rw = _bf(wl[:, o_rw:o_mla])
        w_mla = _bf(jnp.concatenate(
            [wl[:, o_mla:o_rope],
             jnp.tile(wl[:, o_rope:o_rope + half], (1, MLA_HEADS)),
             jnp.tile(wl[:, o_rope + half:o_gate], (1, MLA_HEADS))], axis=1))
        w_gate = _bf(wl[:, o_gate:])

        p_rg = _matmul(xt, w_rg, 512, "proj_rg")
        p_rw = _matmul(xt, w_rw, RW_COLS // 2, "proj_rw")
        p_mla = _matmul(xt, w_mla, w_mla.shape[1], "proj_mla")
        p_gate = _matmul(xt, w_gate, 1024, "proj_gate")

        y_a = _rg_branch(p_rg.reshape(bsz, s, o_rw), rg_conv_w[l], _row(rg_conv_b[l]),
                         _pair_block_diag(rg_wa[l]), _row(rg_ba[l]),
                         _pair_block_diag(rg_wx[l]), _row(rg_bx[l]), _row(rg_log_a[l]))

        zl = jnp.zeros((LORA_W, RW_WIDTH), F32)
        w2p = _bf(jnp.concatenate([rw_w2[l], zl], axis=0))
        a2p = _bf(jnp.concatenate([zl, rw_a2[l]], axis=0))
        vres = None
        if l > 0:
            v1p = _bf(jnp.pad(rw_v1[l - 1], ((0, 0), (0, LANES - LORA_V))))
            v2p = _bf(jnp.pad(rw_v2[l - 1], ((0, LANES - LORA_V), (0, 0))))
            vres = (v_first, _row(rw_v0[l - 1]), v1p, v2p)
        r_, lw_, k_, v_, aa_, bb_, g_ = _rw_prep(
            p_rw.reshape(bsz, s, RW_COLS), _row(rw_mix[l]), _row(rw_w0[l]), w2p, _row(rw_a0[l]), a2p,
            _bf(rw_g2[l]), _row(rw_k_k[l]), _row(rw_k_a[l]), vres)
        if l == 0:
            v_first = v_
        y_b = _rw_scan(r_, lw_, k_, v_, aa_, bb_, g_, _row(rw_r_k[l]), _row(rw_gn_g[l]), _row(rw_gn_b[l]))

        wuq = mla_w_uq[l].reshape(Q_LORA, MLA_HEADS, QK_NOPE + QK_ROPE)
        wuq = _bf(jnp.concatenate([wuq[:, :, :QK_NOPE].reshape(Q_LORA, hn),
                                   wuq[:, :, QK_NOPE:QK_NOPE + half].reshape(Q_LORA, LANES),
                                   wuq[:, :, QK_NOPE + half:].reshape(Q_LORA, LANES)], axis=1))
        wukv = mla_w_ukv[l].reshape(KV_LORA, MLA_HEADS, QK_NOPE + V_HEAD)
        wukv = _bf(jnp.concatenate([wukv[:, :, :QK_NOPE].reshape(KV_LORA, hn),
                                    wukv[:, :, QK_NOPE:].reshape(KV_LORA, hn)], axis=1))
        q_all, k_all, v_all = _mla_proj(p_mla, _row(mla_q_norm[l]), wuq, _row(mla_kv_norm[l]), wukv,
                                        cos_t, sin_t, s)

        def heads(a):
            nope = a[:, :hn].reshape(bsz, s, MLA_HEADS, QK_NOPE)
            pe1 = a[:, hn:hn + LANES].reshape(bsz, s, MLA_HEADS, half)
            pe2 = a[:, hn + LANES:].reshape(bsz, s, MLA_HEADS, half)
            return jnp.transpose(jnp.concatenate([nope, pe1, pe2], axis=-1), (0, 2, 1, 3))

        v_h = jnp.transpose(v_all.reshape(bsz, s, MLA_HEADS, V_HEAD), (0, 2, 1, 3))
        o_h = _attention(heads(q_all), heads(k_all), v_h)
        y_c = jnp.transpose(o_h, (0, 2, 1, 3)).reshape(t, hn)

        x1, x1b = _merge(alpha, y_a.reshape(t, MIX_WIDTH), y_b.reshape(t, MIX_WIDTH), y_c, p_gate, xt,
                         _bf(w_branch[l]), _bf(w_out[l]), _row(ln1_g[l]), _row(ln1_b[l]))

        rk2, e2, ncnt, eps = _peer_a(x1b, _bf(peer_w_query[l]), _bf(peer_subkeys[l, 0]),
                                     _bf(peer_subkeys[l, 1]))
        xt = _peer_b(alpha, x1b, _bf(peer_u[l]), _bf(jnp.transpose(peer_v[l])), rk2, e2, ncnt, eps, x1,
                     _row(ln2_g[l]), _row(ln2_b[l]))
    return xt.reshape(bsz, s, d)
```

```python
import functools
import math

import jax
import jax.numpy as jnp
from jax import lax
from jax.experimental import pallas as pl
from jax.experimental.pallas import tpu as pltpu

F32 = jnp.float32
BF16 = jnp.bfloat16

N_BRANCH = 3
MIX_WIDTH = 512
RG_WIDTH = 512
RG_BLOCK = 64
CONV_WIDTH = 4
RG_C = 8.0
RW_HEADS = 8
RW_HEAD = 64
RW_WIDTH = 512
LORA_W = 64
LORA_A = 64
LORA_V = 32
LORA_G = 128
RW_GN_EPS = 64e-5
RW_COLS = 3 * RW_WIDTH + LORA_W + LORA_A + LORA_G
MLA_HEADS = 8
QK_NOPE = 64
QK_ROPE = 32
V_HEAD = 64
Q_LORA = 256
KV_LORA = 128
ROPE_THETA = 10000.0
MLA_COLS = Q_LORA + KV_LORA + QK_ROPE
N_KEYS = 128
PEER_HEADS = 8
PEER_HALF = 128
PEER_TOPK = 16
LN_EPS = 1e-5

LANES = 128
V7X_VMEM_BYTES = 64 * 1024 * 1024
MIB = 1024 * 1024

MM_TM = 1024
RW_TS = 256
RW_CHUNK = 64
MLA_TM = 512
ATT_TQ = 256
ATT_KB = 512
MERGE_TM = 512
PEERA_TM = 256
PEERB_TM = 512
PEERB_EC = 512


def _cp(sem, vmem_mib):
    return pltpu.CompilerParams(dimension_semantics=sem, vmem_limit_bytes=vmem_mib * MIB)


def _bf(x):
    return x.astype(BF16)


def _dot(a, b):
    return jnp.dot(a, b, preferred_element_type=F32)


def _dot_nt(a, b):
    return lax.dot_general(a, b, (((1,), (1,)), ((), ())), preferred_element_type=F32)


def _split(x):
    hi = x.astype(BF16)
    lo = (x - hi.astype(F32)).astype(BF16)
    return hi, lo


def _dot_hilo(x, w_bf):
    hi, lo = _split(x)
    return _dot(hi, w_bf) + _dot(lo, w_bf)


def _dot3(a, b):
    ah, al = _split(a)
    bh, bl = _split(b)
    return _dot(ah, bh) + (_dot(ah, bl) + _dot(al, bh))


def _shift_rows(x, k, fill, rows):
    return jnp.where(rows >= k, pltpu.roll(x, k, 0), fill)


def _log1p(u):
    w = 1.0 + u
    d = w - 1.0
    return jnp.where(d == 0.0, u, jnp.log(w) * (u / jnp.where(d == 0.0, 1.0, d)))


def _softplus(z):
    return jnp.maximum(z, 0.0) + _log1p(jnp.exp(-jnp.abs(z)))


def _neg_expm1(x):
    u = jnp.exp(x)
    lu = jnp.log(u)
    safe = jnp.where(lu == 0.0, 1.0, lu)
    r = (1.0 - u) * x / safe
    r = jnp.where(u == 1.0, -x, r)
    return jnp.where(u == 0.0, 1.0, r)


def _layer_norm(z, g, b):
    mu = jnp.mean(z, axis=-1, keepdims=True)
    d = z - mu
    var = jnp.mean(d * d, axis=-1, keepdims=True)
    return d * lax.rsqrt(var + LN_EPS) * g + b


def _mm_kernel(x_ref, w_ref, o_ref):
    o_ref[...] = _dot(_bf(x_ref[...]), w_ref[...]).astype(o_ref.dtype)


def _matmul(x, w, tn, name, out_dtype=F32):
    m, k = x.shape
    n = w.shape[1]
    tm = min(MM_TM, m)
    return pl.pallas_call(
        _mm_kernel,
        grid=(m // tm, n // tn),
        in_specs=[pl.BlockSpec((tm, k), lambda i, j: (i, 0)),
                  pl.BlockSpec((k, tn), lambda i, j: (0, j))],
        out_specs=pl.BlockSpec((tm, tn), lambda i, j: (i, j)),
        out_shape=jax.ShapeDtypeStruct((m, n), out_dtype),
        name=name,
        compiler_params=_cp(("parallel", "parallel"), 40),
    )(x, w)


def _rg_kernel(xb_ref, gb_ref, cw_ref, cb_ref, wa_ref, ba_ref, wx_ref, bx_ref, la_ref, o_ref):
    x = xb_ref[0]
    s = x.shape[0]
    rows = lax.broadcasted_iota(jnp.int32, x.shape, 0)
    cw = cw_ref[...]
    xc = cb_ref[...] + cw[0:1, :] * x
    for j in range(1, CONV_WIDTH):
        xc = xc + cw[j:j + 1, :] * _shift_rows(x, j, 0.0, rows)
    xcb = _bf(xc)
    r = jax.nn.sigmoid(_dot(xcb, wa_ref[0]) + ba_ref[...])
    i = jax.nn.sigmoid(_dot(xcb, wx_ref[0]) + bx_ref[...])
    log_a = (-RG_C) * r * _softplus(-la_ref[...])
    a = jnp.exp(log_a)
    mult = jnp.where(rows == 0, 1.0, jnp.sqrt(_neg_expm1(2.0 * log_a)))
    b = mult * (i * xc)
    k = 1
    while k < s:
        a_s = _shift_rows(a, k, 1.0, rows)
        b_s = _shift_rows(b, k, 0.0, rows)
        b = a * b_s + b
        a = a * a_s
        k *= 2
    o_ref[0] = b * jax.nn.gelu(gb_ref[0])


def _rg_branch(p_rg, cw, cb, wa_bd, ba, wx_bd, bx, la):
    bsz, s, _ = p_rg.shape
    nct = RG_WIDTH // LANES
    vec = lambda: pl.BlockSpec((1, LANES), lambda b, c: (0, c))
    return pl.pallas_call(
        _rg_kernel,
        grid=(bsz, nct),
        in_specs=[pl.BlockSpec((1, s, LANES), lambda b, c: (b, 0, c)),
                  pl.BlockSpec((1, s, LANES), lambda b, c: (b, 0, c + nct)),
                  pl.BlockSpec((CONV_WIDTH, LANES), lambda b, c: (0, c)),
                  vec(),
                  pl.BlockSpec((1, LANES, LANES), lambda b, c: (c, 0, 0)),
                  vec(),
                  pl.BlockSpec((1, LANES, LANES), lambda b, c: (c, 0, 0)),
                  vec(), vec()],
        out_specs=pl.BlockSpec((1, s, LANES), lambda b, c: (b, 0, c)),
        out_shape=jax.ShapeDtypeStruct((bsz, s, RG_WIDTH), F32),
        name="rg_lru",
        compiler_params=_cp(("parallel", "parallel"), 48),
    )(p_rg, p_rg, cw, cb, wa_bd, ba, wx_bd, bx, la)


def _head_ones(n):
    sh = jnp.int32(RW_HEAD.bit_length() - 1)
    ri = lax.shift_right_logical(lax.broadcasted_iota(jnp.int32, (n, n), 0), sh)
    ci = lax.shift_right_logical(lax.broadcasted_iota(jnp.int32, (n, n), 1), sh)
    return jnp.where(ri == ci, 1.0, 0.0).astype(BF16)


def _rw_prep_kernel(has_vres, pb_ref, prev_ref, mix_ref, w0_ref, w2_ref, a0_ref, a2_ref, g2_ref,
                    kk_ref, ka_ref, *rest):
    if has_vres:
        vf_ref, v0_ref, v1_ref, v2_ref = rest[:4]
        rest = rest[4:]
    r_o, lw_o, k_o, v_o, aa_o, bb_o, g_o = rest
    t = pl.program_id(1)
    p = pb_ref[0]
    rows = lax.broadcasted_iota(jnp.int32, p.shape, 0)
    prev = jnp.where(t == 0, 0.0, prev_ref[0][7:8, :])
    shifted = jnp.where(rows == 0, prev, pltpu.roll(p, 1, 0))
    p = p + (shifted - p) * mix_ref[...]
    w = RW_WIDTH
    r = p[:, 0:w]
    k = p[:, w:2 * w]
    v = p[:, 2 * w:3 * w]
    xwa = p[:, 3 * w:3 * w + LORA_W + LORA_A]
    xg = p[:, 3 * w + LORA_W + LORA_A:]
    wl = -_softplus(-(w0_ref[...] + _dot(_bf(jnp.tanh(xwa)), w2_ref[...]))) - 0.5
    a = jax.nn.sigmoid(a0_ref[...] + _dot(_bf(xwa), a2_ref[...]))
    g = _dot(_bf(jax.nn.sigmoid(xg)), g2_ref[...])
    if has_vres:
        lo = _dot(_bf(v), v1_ref[...])
        v = v + (vf_ref[0] - v) * jax.nn.sigmoid(v0_ref[...] + _dot(_bf(lo), v2_ref[...]))
    kk = k * kk_ref[...]
    ss = _dot_hilo(kk * kk, _head_ones(w))
    kk = kk / jnp.maximum(jnp.sqrt(ss), 1e-12)
    r_o[0] = r
    lw_o[0] = -jnp.exp(wl)
    k_o[0] = k * (1.0 + (a - 1.0) * ka_ref[...])
    v_o[0] = v
    aa_o[0] = -kk
    bb_o[0] = kk * a
    g_o[0] = g


def _rw_prep(pb, mix, w0, w2p, a0, a2p, g2, k_k, k_a, vres):
    bsz, s, cols = pb.shape
    ts = min(RW_TS, s)
    w = RW_WIDTH
    full = lambda shape: pl.BlockSpec(shape, lambda b, t: (0,) * len(shape))
    tok = lambda c: pl.BlockSpec((1, ts, c), lambda b, t: (b, t, 0))
    in_specs = [tok(cols),
                pl.BlockSpec((1, 8, cols), lambda b, t: (b, jnp.maximum(t * (ts // 8) - 1, 0), 0)),
                full((1, cols)), full((1, w)), full((LANES, w)), full((1, w)), full((LANES, w)),
                full((LORA_G, w)), full((1, w)), full((1, w))]
    args = [pb, pb, mix, w0, w2p, a0, a2p, g2, k_k, k_a]
    if vres is not None:
        v_first, v0, v1p, v2p = vres
        in_specs += [tok(w), full((1, w)), full((w, LANES)), full((LANES, w))]
        args += [v_first, v0, v1p, v2p]
    out = jax.ShapeDtypeStruct((bsz, s, w), F32)
    return pl.pallas_call(
        functools.partial(_rw_prep_kernel, vres is not None),
        grid=(bsz, s // ts),
        in_specs=in_specs,
        out_specs=[tok(w)] * 7,
        out_shape=[out] * 7,
        name="rwkv_prep",
        compiler_params=_cp(("parallel", "parallel"), 40),
    )(*args)


def _rw_scan_kernel(r_ref, lw_ref, k_ref, v_ref, aa_ref, bb_ref, g_ref, rk_ref, gng_ref, gnb_ref,
                    o_ref, st_ref):
    c = pl.program_id(1)

    @pl.when(c == 0)
    def _():
        st_ref[...] = jnp.zeros_like(st_ref)

    ch = r_ref.shape[1]
    c2 = 2 * ch
    rows = lax.broadcasted_iota(jnp.int32, (ch, LANES), 0)
    head0 = lax.broadcasted_iota(jnp.int32, (ch, LANES), 1) < RW_HEAD
    ri = lax.broadcasted_iota(jnp.int32, (c2, c2), 0)
    ci = lax.broadcasted_iota(jnp.int32, (c2, c2), 1)
    blk_lo = jnp.where(ri >= ch, ch, 0)
    in_blk = jnp.where(ci >= blk_lo, jnp.where(ci < blk_lo + ch, 1.0, 0.0), 0.0)
    strict = jnp.where(ci < ri, in_blk, 0.0)
    incl = jnp.where(ci <= ri, in_blk, 0.0)
    eye2 = jnp.where(ci == ri, 1.0, 0.0)
    li = lax.broadcasted_iota(jnp.int32, (LANES, LANES), 0)
    lj = lax.broadcasted_iota(jnp.int32, (LANES, LANES), 1)
    eye_l = li == lj
    ones_hd = _head_ones(LANES)

    def stack(x):
        return jnp.concatenate([jnp.where(head0, x, 0.0), jnp.where(head0, 0.0, x)], axis=0)

    pairs = range(RW_WIDTH // LANES)
    sls = [slice(p * LANES, (p + 1) * LANES) for p in pairs]
    pre = []
    for sl in sls:
        lw = lw_ref[0][:, sl]
        cum = lw
        kk = 1
        while kk < ch:
            cum = cum + _shift_rows(cum, kk, 0.0, rows)
            kk *= 2
        tot = cum[ch - 1:ch, :]
        e_in = jnp.exp(cum)
        e_out = jnp.exp(-cum)
        e_ex = jnp.exp(cum - lw)
        e_end = jnp.exp(tot - cum)
        r = r_ref[0][:, sl]
        k = k_ref[0][:, sl]
        v = v_ref[0][:, sl]
        aa = aa_ref[0][:, sl]
        bb = bb_ref[0][:, sl]
        pre.append(dict(
            r=r, k=k, v=v, tot=tot,
            a_s=stack(aa * e_ex), r_s=stack(r * e_in), b_s=stack(bb * e_out), k_s=stack(k * e_out),
            v_sb=_bf(stack(v)), bh_t=_bf(stack(bb * e_end).T), kh_t=_bf(stack(k * e_end).T)))
    grams = [_dot_nt(_bf(jnp.concatenate([q["a_s"], q["r_s"]], axis=0)),
                     _bf(jnp.concatenate([q["b_s"], q["k_s"]], axis=0))) for q in pre]
    l_ab = [g[:c2, :c2] * strict for g in grams]
    l_ak = [_bf(g[:c2, c2:] * strict) for g in grams]
    m_rb = [_bf(g[c2:, :c2] * incl) for g in grams]
    m_rk = [_bf(g[c2:, c2:] * incl) for g in grams]
    pw = l_ab
    tinv = [eye2 + x for x in l_ab]
    n = 2
    while n < ch:
        pwb = [_bf(x) for x in pw]
        pw = [_dot(x, x) for x in pwb]
        tinv = [t + _dot(_bf(t), _bf(x)) for t, x in zip(tinv, pw)]
        n *= 2
    lakv = [_dot(x, q["v_sb"]) for x, q in zip(l_ak, pre)]
    wb = [_bf(_dot(_bf(t), _bf(jnp.concatenate([q["a_s"], x], axis=1))))
          for t, q, x in zip(tinv, pre, lakv)]
    qy = [_dot(m, w) for m, w in zip(m_rb, wb)]
    mv = [_dot(m, q["v_sb"]) for m, q in zip(m_rk, pre)]
    ac = [_dot(q["bh_t"], w) for q, w in zip(pre, wb)]
    kv = [_dot(q["kh_t"], q["v_sb"]) for q in pre]
    ys = []
    for p in pairs:
        q_hat = pre[p]["r_s"] + qy[p][:, :LANES]
        y_loc = qy[p][:, LANES:] + mv[p]
        a_hat = jnp.where(eye_l, jnp.exp(pre[p]["tot"]), 0.0) + ac[p][:, :LANES]
        c_hat = ac[p][:, LANES:] + kv[p]
        st = st_ref[p]
        y_s = _dot3(q_hat, st) + y_loc
        st_ref[p] = _dot3(a_hat, st) + c_hat
        ys.append(y_s[:ch] + y_s[ch:])
    mus = [_dot_hilo(y, ones_hd) * (1.0 / RW_HEAD) for y in ys]
    ds = [y - m for y, m in zip(ys, mus)]
    vrs = [_dot_hilo(d * d, ones_hd) * (1.0 / RW_HEAD) for d in ds]
    bns = [_dot_hilo(q["r"] * q["k"] * rk_ref[:, sl], ones_hd) for q, sl in zip(pre, sls)]
    for p, sl in zip(pairs, sls):
        yn = ds[p] * lax.rsqrt(vrs[p] + RW_GN_EPS) * gng_ref[:, sl] + gnb_ref[:, sl]
        o_ref[0, :, sl] = (yn + bns[p] * pre[p]["v"]) * g_ref[0][:, sl]


def _rw_scan(r, lw, k, v, aa, bb, g, r_k, gn_g, gn_b):
    bsz, s, w = r.shape
    ch = min(RW_CHUNK, s)
    tok = pl.BlockSpec((1, ch, w), lambda b, c: (b, c, 0))
    vec = pl.BlockSpec((1, w), lambda b, c: (0, 0))
    return pl.pallas_call(
        _rw_scan_kernel,
        grid=(bsz, s // ch),
        in_specs=[tok] * 7 + [vec] * 3,
        out_specs=tok,
        out_shape=jax.ShapeDtypeStruct((bsz, s, w), F32),
        scratch_shapes=[pltpu.VMEM((w // LANES, LANES, LANES), F32)],
        name="rwkv_scan",
        compiler_params=_cp(("parallel", "arbitrary"), 40),
    )(r, lw, k, v, aa, bb, g, r_k, gn_g, gn_b)


def _rms(x, g):
    return x * lax.rsqrt(jnp.mean(x * x, axis=-1, keepdims=True) + 1e-6) * g


def _mla_proj_kernel(pm_ref, qn_ref, wuq_ref, kvn_ref, wukv_ref, cos_ref, sin_ref, q_o, k_o, v_o):
    c = pm_ref[...]
    hn = MLA_HEADS * QK_NOPE
    q = _dot(_bf(_rms(c[:, :Q_LORA], qn_ref[...])), wuq_ref[...])
    kv = _dot(_bf(_rms(c[:, Q_LORA:Q_LORA + KV_LORA], kvn_ref[...])), wukv_ref[...])
    k1 = c[:, Q_LORA + KV_LORA:Q_LORA + KV_LORA + LANES]
    k2 = c[:, Q_LORA + KV_LORA + LANES:]
    cos = cos_ref[...]
    sin = sin_ref[...]
    q1 = q[:, hn:hn + LANES]
    q2 = q[:, hn + LANES:]
    q_o[:, :hn] = _bf(q[:, :hn])
    q_o[:, hn:hn + LANES] = _bf(q1 * cos - q2 * sin)
    q_o[:, hn + LANES:] = _bf(q1 * sin + q2 * cos)
    k_o[:, :hn] = _bf(kv[:, :hn])
    k_o[:, hn:hn + LANES] = _bf(k1 * cos - k2 * sin)
    k_o[:, hn + LANES:] = _bf(k1 * sin + k2 * cos)
    v_o[...] = _bf(kv[:, hn:])


def _mla_proj(pm, q_norm, wuq, kv_norm, wukv, cos_t, sin_t, s):
    t, cols = pm.shape
    tm = min(MLA_TM, s)
    nst = s // tm
    hn = MLA_HEADS * QK_NOPE
    qw = hn + 2 * LANES
    full = lambda shape: pl.BlockSpec(shape, lambda i: (0,) * len(shape))
    tok = lambda c: pl.BlockSpec((tm, c), lambda i: (i, 0))
    pos = pl.BlockSpec((tm, LANES), lambda i: (i % nst, 0))
    return pl.pallas_call(
        _mla_proj_kernel,
        grid=(t // tm,),
        in_specs=[tok(cols), full((1, Q_LORA)), full((Q_LORA, qw)), full((1, KV_LORA)),
                  full((KV_LORA, 2 * hn)), pos, pos],
        out_specs=[tok(qw), tok(qw), tok(hn)],
        out_shape=[jax.ShapeDtypeStruct((t, qw), BF16), jax.ShapeDtypeStruct((t, qw), BF16),
                   jax.ShapeDtypeStruct((t, hn), BF16)],
        name="mla_proj",
        compiler_params=_cp(("parallel",), 32),
    )(pm, q_norm, wuq, kv_norm, wukv, cos_t, sin_t)


def _attn_kernel(q_ref, k_ref, v_ref, o_ref, *, scale):
    i = pl.program_id(2)
    tq = q_ref.shape[2]
    s_len = k_ref.shape[2]
    kb = min(ATT_KB, s_len)
    need = ((i + 1) * tq + kb - 1) // kb
    for n in range(1, s_len // kb + 1):
        @pl.when(need == n)
        def _(n=n):
            q = q_ref[0, 0]
            s = _dot_nt(q, k_ref[0, 0, :n * kb, :]) * scale
            qpos = i * tq + lax.broadcasted_iota(jnp.int32, s.shape, 0)
            kpos = lax.broadcasted_iota(jnp.int32, s.shape, 1)
            s = jnp.where(kpos <= qpos, s, -1e30)
            m = jnp.max(s, axis=-1, keepdims=True)
            p = jnp.exp(s - m)
            l = jnp.sum(p, axis=-1, keepdims=True)
            o = _dot(_bf(p), v_ref[0, 0, :n * kb, :]) / l
            o_ref[0, 0] = o.astype(o_ref.dtype)


def _attention(q, k, v):
    bsz, h, s, dq = q.shape
    dv = v.shape[-1]
    tq = min(ATT_TQ, s)
    return pl.pallas_call(
        functools.partial(_attn_kernel, scale=(QK_NOPE + QK_ROPE) ** -0.5),
        grid=(bsz, h, s // tq),
        in_specs=[pl.BlockSpec((1, 1, tq, dq), lambda b, hh, i: (b, hh, i, 0)),
                  pl.BlockSpec((1, 1, s, dq), lambda b, hh, i: (b, hh, 0, 0)),
                  pl.BlockSpec((1, 1, s, dv), lambda b, hh, i: (b, hh, 0, 0))],
        out_specs=pl.BlockSpec((1, 1, tq, dv), lambda b, hh, i: (b, hh, i, 0)),
        out_shape=jax.ShapeDtypeStruct((bsz, h, s, dv), BF16),
        name="mla_attention",
        compiler_params=_cp(("parallel", "parallel", "parallel"), 40),
    )(q, k, v)


def _merge_kernel(alpha, ya_ref, yb_ref, yc_ref, pg_ref, x_ref, wb_ref, wo_ref, g_ref, b_ref,
                  o_ref, ob_ref):
    d = x_ref.shape[1]
    merged = jax.nn.sigmoid(pg_ref[:, 0:d]) * _dot(_bf(ya_ref[...]), wb_ref[0])
    merged = merged + jax.nn.sigmoid(pg_ref[:, d:2 * d]) * _dot(_bf(yb_ref[...]), wb_ref[1])
    merged = merged + jax.nn.sigmoid(pg_ref[:, 2 * d:3 * d]) * _dot(_bf(yc_ref[...]), wb_ref[2])
    z = alpha * x_ref[...] + _dot(_bf(merged), wo_ref[...])
    out = _layer_norm(z, g_ref[...], b_ref[...])
    o_ref[...] = out
    ob_ref[...] = _bf(out)


def _merge(alpha, ya, yb, yc, pg, x, w_branch, w_out, g, b):
    t, d = x.shape
    tm = min(MERGE_TM, t)
    tok = lambda c: pl.BlockSpec((tm, c), lambda i: (i, 0))
    full = lambda shape: pl.BlockSpec(shape, lambda i: (0,) * len(shape))
    return pl.pallas_call(
        functools.partial(_merge_kernel, alpha),
        grid=(t // tm,),
        in_specs=[tok(MIX_WIDTH), tok(MIX_WIDTH), tok(MIX_WIDTH), tok(N_BRANCH * d), tok(d),
                  full((N_BRANCH, MIX_WIDTH, d)), full((d, d)), full((1, d)), full((1, d))],
        out_specs=[tok(d), tok(d)],
        out_shape=[jax.ShapeDtypeStruct((t, d), F32), jax.ShapeDtypeStruct((t, d), BF16)],
        name="merge_ln",
        compiler_params=_cp(("parallel",), 48),
    )(ya, yb, yc, pg, x, w_branch, w_out, g, b)


def _top16(x, kidx):
    rank = jnp.full(x.shape, float(PEER_TOPK), F32)
    vals = []
    nk = float(x.shape[0])
    for b in range(PEER_TOPK):
        m = jnp.max(x, axis=0, keepdims=True)
        first = jnp.min(jnp.where(x == m, kidx, nk), axis=0, keepdims=True)
        hit = kidx == first
        rank = jnp.where(hit, float(b), rank)
        x = jnp.where(hit, -jnp.inf, x)
        vals.append(m)
    return jnp.concatenate(vals, axis=0), rank


def _peer_a_kernel(x_ref, wq_ref, k1_ref, k2_ref, rk2_o, e2_o, n_o, eps_o, q_scr):
    q_scr[...] = _bf(_dot(x_ref[...], wq_ref[...]))
    tm = x_ref.shape[0]
    kidx = lax.broadcasted_iota(jnp.int32, (N_KEYS, tm), 0).astype(F32)
    aidx = lax.broadcasted_iota(jnp.int32, (PEER_TOPK, tm), 0).astype(F32)

    def head(h, carry):
        base = pl.multiple_of(h * (2 * PEER_HALF), 2 * PEER_HALF)
        s1 = _dot_nt(k1_ref[...], q_scr[:, pl.ds(base, PEER_HALF)])
        s2 = _dot_nt(k2_ref[...], q_scr[:, pl.ds(base + PEER_HALF, PEER_HALF)])
        v1, rk1 = _top16(s1, kidx)
        v2, rk2 = _top16(s2, kidx)
        cnt = jnp.zeros((PEER_TOPK, tm), F32)
        front = v1 + v2[0:1, :]
        for _ in range(PEER_TOPK):
            m = jnp.max(front, axis=0, keepdims=True)
            a_star = jnp.min(jnp.where(front == m, aidx, float(PEER_TOPK)), axis=0, keepdims=True)
            hit = aidx == a_star
            cnt = cnt + jnp.where(hit, 1.0, 0.0)
            nxt = jnp.sum(jnp.where(hit, cnt, 0.0), axis=0, keepdims=True)
            v2n = jnp.max(jnp.where(aidx == nxt, v2, -jnp.inf), axis=0, keepdims=True)
            front = jnp.where(hit, v1 + v2n, front)
        e1 = jnp.exp(v1 - v1[0:1, :])
        e2 = jnp.exp(v2 - v2[0:1, :])
        pref = jnp.zeros((PEER_TOPK, tm), F32)
        for b in range(PEER_TOPK):
            pref = pref + jnp.where(cnt > float(b), e2[b:b + 1, :], 0.0)
        z = jnp.sum(e1 * pref, axis=0, keepdims=True)
        ncount = jnp.zeros((N_KEYS, tm), F32)
        for a in range(PEER_TOPK):
            ncount = jnp.where(rk1 == float(a), cnt[a:a + 1, :], ncount)
        rk2_o[h] = _bf(rk2)
        e2_o[h] = _bf(jnp.exp(s2 - v2[0:1, :]))
        n_o[h] = ncount
        eps_o[h] = jnp.exp(s1 - v1[0:1, :]) / z
        return carry

    lax.fori_loop(0, PEER_HEADS, head, 0)


def _peer_a(xb, wq, k1, k2):
    t, d = xb.shape
    tm = min(PEERA_TM, t)
    qd = wq.shape[1]
    full = lambda shape: pl.BlockSpec(shape, lambda i: (0,) * len(shape))
    tab = pl.BlockSpec((PEER_HEADS, N_KEYS, tm), lambda i: (0, 0, i))
    out = jax.ShapeDtypeStruct((PEER_HEADS, N_KEYS, t), F32)
    out_bf = jax.ShapeDtypeStruct((PEER_HEADS, N_KEYS, t), BF16)
    return pl.pallas_call(
        _peer_a_kernel,
        grid=(t // tm,),
        in_specs=[pl.BlockSpec((tm, d), lambda i: (i, 0)), full((d, qd)),
                  full((N_KEYS, PEER_HALF)), full((N_KEYS, PEER_HALF))],
        out_specs=[tab] * 4,
        out_shape=[out_bf, out_bf, out, out],
        scratch_shapes=[pltpu.VMEM((tm, qd), BF16)],
        name="peer_topk",
        compiler_params=_cp(("parallel",), 40),
    )(xb, wq, k1, k2)


def _peer_b_kernel(alpha, xb_ref, u0_ref, uo_ref, un_ref, vtp_ref, vte_ref, vtl_ref,
                   rk2_ref, e2_ref, n_ref, eps_ref, x_ref, g_ref, b_ref,
                   o_ref, acc_ref, acta_ref, actb_ref, ha_ref, hb_ref):
    g = pl.program_id(1)
    nblk = uo_ref.shape[0] // N_KEYS
    xb = xb_ref[...]

    @pl.when(g == 0)
    def _():
        acc_ref[...] = jnp.zeros_like(acc_ref)
        hb_ref[...] = jnp.zeros_like(hb_ref)
        acta_ref[...] = _dot_nt(u0_ref[...], xb)

    def phase(half, act_ref, h_out_ref, vt_ref, h_in_ref, u_ref, act_next_ref):
        acc_ref[...] += _dot(vt_ref[...], h_in_ref[...])
        act_next_ref[...] = _dot_nt(u_ref[...], xb)
        for il in range(nblk):
            i = half * nblk + il
            rows = slice(il * N_KEYS, (il + 1) * N_KEYS)
            gate = None
            for h in range(PEER_HEADS):
                cnt = _bf(n_ref[h, i:i + 1, :])
                eps = _bf(eps_ref[h, i:i + 1, :])
                term = jnp.where(rk2_ref[h] < cnt, e2_ref[h] * eps, jnp.zeros((), BF16))
                gate = term if gate is None else gate + term
            h_out_ref[rows, :] = _bf(jax.nn.gelu(act_ref[rows, :])) * gate

    phase(0, acta_ref, ha_ref, vtp_ref, hb_ref, uo_ref, actb_ref)
    phase(1, actb_ref, hb_ref, vte_ref, ha_ref, un_ref, acta_ref)

    @pl.when(g == pl.num_programs(1) - 1)
    def _():
        acc = acc_ref[...] + _dot(vtl_ref[...], hb_ref[...])
        z = alpha * x_ref[...] + acc.T
        o_ref[...] = _layer_norm(z, g_ref[...], b_ref[...])


def _peer_b(alpha, xb, u_bf, vt_bf, rk2, e2, ncnt, eps, x, g, b):
    t, d = x.shape
    ne = u_bf.shape[0]
    tm = min(PEERB_TM, t)
    ec = min(PEERB_EC, ne)
    nch = ne // ec
    assert nch % 2 == 0
    tab = pl.BlockSpec((PEER_HEADS, N_KEYS, tm), lambda i, j: (0, 0, i))
    rowtab = pl.BlockSpec((PEER_HEADS, 2 * ec // N_KEYS, tm), lambda i, j: (0, j, i))
    vec = pl.BlockSpec((1, d), lambda i, j: (0, 0))
    ublk =lambda f: pl.BlockSpec((ec, d), lambda i, j: (f(j), 0))
    vblk = lambda f: pl.BlockSpec((d, ec), lambda i, j: (0, f(j)))
    return pl.pallas_call(
        functools.partial(_peer_b_kernel, alpha),
        grid=(t // tm, nch // 2),
        in_specs=[pl.BlockSpec((tm, d), lambda i, j: (i, 0)),
                  ublk(lambda j: 0), ublk(lambda j: 2 * j + 1),
                  ublk(lambda j: jnp.minimum(2 * j + 2, nch - 1)),
                  vblk(lambda j: jnp.maximum(2 * j - 1, 0)), vblk(lambda j: 2 * j),
                  vblk(lambda j: nch - 1),
                  tab, tab, rowtab, rowtab,
                  pl.BlockSpec((tm, d), lambda i, j: (i, 0)), vec, vec],
        out_specs=pl.BlockSpec((tm, d), lambda i, j: (i, 0)),
        out_shape=jax.ShapeDtypeStruct((t, d), F32),
        scratch_shapes=[pltpu.VMEM((d, tm), F32), pltpu.VMEM((ec, tm), F32), pltpu.VMEM((ec, tm), F32),
                        pltpu.VMEM((ec, tm), BF16), pltpu.VMEM((ec, tm), BF16)],
        name="peer_dense",
        compiler_params=_cp(("parallel", "arbitrary"), 52),
    )(xb, u_bf, u_bf, u_bf, vt_bf, vt_bf, vt_bf, rk2, e2, ncnt, eps, x, g, b)


def _pair_block_diag(w):
    nb = w.shape[0]
    w4 = w.reshape(nb // 2, 2, RG_BLOCK, RG_BLOCK)
    z = jnp.zeros_like(w4[:, 0])
    top = jnp.concatenate([w4[:, 0], z], axis=-1)
    bot = jnp.concatenate([z, w4[:, 1]], axis=-1)
    return _bf(jnp.concatenate([top, bot], axis=-2))


def _row(v):
    return v.reshape(1, -1)


def kernel(x, w_in, rg_conv_w, rg_conv_b, rg_wa, rg_ba, rg_wx, rg_bx, rg_log_a, rw_mix, rw_w0, rw_w2, rw_a0, rw_a2, rw_g2, rw_v0, rw_v1, rw_v2, rw_k_k, rw_k_a, rw_r_k, rw_gn_g, rw_gn_b, mla_q_norm, mla_w_uq, mla_kv_norm, mla_w_ukv, w_branch, w_out, ln1_g, ln1_b, peer_w_query, peer_subkeys, peer_u, peer_v, ln2_g, ln2_b):
    bsz, s, d = x.shape
    depth = w_in.shape[0]
    t = bsz * s
    alpha = (2.0 * depth) ** 0.25
    hn = MLA_HEADS * QK_NOPE
    half = QK_ROPE // 2

    pos = jnp.arange(s, dtype=F32)
    inv_freq = ROPE_THETA ** (-jnp.arange(half, dtype=F32) / half)
    ang = pos[:, None] * inv_freq[None, :]
    cos_t = jnp.tile(jnp.cos(ang).astype(F32), (1, MLA_HEADS))
    sin_t = jnp.tile(jnp.sin(ang).astype(F32), (1, MLA_HEADS))

    xt = x.reshape(t, d)
    v_first = None
    for l in range(depth):
        wl = w_in[l]
        o_rw = 2 * RG_WIDTH
        o_mla = o_rw + RW_COLS
        o_gate = o_mla + MLA_COLS
        o_rope = o_mla + Q_LORA + KV_LORA
        w_rg = _bf(wl[:, :o_rw])
        w_rw = _bf(wl[:, o_rw:o_mla])
        w_mla = _bf(jnp.concatenate(
            [wl[:, o_mla:o_rope],
             jnp.tile(wl[:, o_rope:o_rope + half], (1, MLA_HEADS)),
             jnp.tile(wl[:, o_rope + half:o_gate], (1, MLA_HEADS))], axis=1))
        w_gate = _bf(wl[:, o_gate:])

        p_rg = _matmul(xt, w_rg, 512, "proj_rg")
        p_rw = _matmul(xt, w_rw, RW_COLS // 2, "proj_rw")
        p_mla = _matmul(xt, w_mla, w_mla.shape[1], "proj_mla")
        p_gate = _matmul(xt, w_gate, 1024, "proj_gate")

        y_a = _rg_branch(p_rg.reshape(bsz, s, o_rw), rg_conv_w[l], _row(rg_conv_b[l]),
                         _pair_block_diag(rg_wa[l]), _row(rg_ba[l]),
                         _pair_block_diag(rg_wx[l]), _row(rg_bx[l]), _row(rg_log_a[l]))

        zl = jnp.zeros((LORA_W, RW_WIDTH), F32)
        w2p = _bf(jnp.concatenate([rw_w2[l], zl], axis=0))
        a2p = _bf(jnp.concatenate([zl, rw_a2[l]], axis=0))
        vres = None
        if l > 0:
            v1p = _bf(jnp.pad(rw_v1[l - 1], ((0, 0), (0, LANES - LORA_V))))
            v2p = _bf(jnp.pad(rw_v2[l - 1], ((0, LANES - LORA_V), (0, 0))))
            vres = (v_first, _row(rw_v0[l - 1]), v1p, v2p)
        r_, lw_, k_, v_, aa_, bb_, g_ = _rw_prep(
            p_rw.reshape(bsz, s, RW_COLS), _row(rw_mix[l]), _row(rw_w0[l]), w2p, _row(rw_a0[l]), a2p,
            _bf(rw_g2[l]), _row(rw_k_k[l]), _row(rw_k_a[l]), vres)
        if l == 0:
            v_first = v_
        y_b = _rw_scan(r_, lw_, k_, v_, aa_, bb_, g_, _row(rw_r_k[l]), _row(rw_gn_g[l]), _row(rw_gn_b[l]))

        wuq = mla_w_uq[l].reshape(Q_LORA, MLA_HEADS, QK_NOPE + QK_ROPE)
        wuq = _bf(jnp.concatenate([wuq[:, :, :QK_NOPE].reshape(Q_LORA, hn),
                                   wuq[:, :, QK_NOPE:QK_NOPE + half].reshape(Q_LORA, LANES),
                                   wuq[:, :, QK_NOPE + half:].reshape(Q_LORA, LANES)], axis=1))
        wukv = mla_w_ukv[l].reshape(KV_LORA, MLA_HEADS, QK_NOPE + V_HEAD)
        wukv = _bf(jnp.concatenate([wukv[:, :, :QK_NOPE].reshape(KV_LORA, hn),
                                    wukv[:, :, QK_NOPE:].reshape(KV_LORA, hn)], axis=1))
        q_all, k_all, v_all = _mla_proj(p_mla, _row(mla_q_norm[l]), wuq, _row(mla_kv_norm[l]), wukv,
                                        cos_t, sin_t, s)

        def heads(a):
            nope = a[:, :hn].reshape(bsz, s, MLA_HEADS, QK_NOPE)
            pe1 = a[:, hn:hn + LANES].reshape(bsz, s, MLA_HEADS, half)
            pe2 = a[:, hn + LANES:].reshape(bsz, s, MLA_HEADS, half)
            return jnp.transpose(jnp.concatenate([nope, pe1, pe2], axis=-1), (0, 2, 1, 3))

        v_h = jnp.transpose(v_all.reshape(bsz, s, MLA_HEADS, V_HEAD), (0, 2, 1, 3))
        o_h = _attention(heads(q_all), heads(k_all), v_h)
        y_c = jnp.transpose(o_h, (0, 2, 1, 3)).reshape(t, hn)

        x1, x1b = _merge(alpha, y_a.reshape(t, MIX_WIDTH), y_b.reshape(t, MIX_WIDTH), y_c, p_gate, xt,
                         _bf(w_branch[l]), _bf(w_out[l]), _row(ln1_g[l]), _row(ln1_b[l]))

        rk2, e2, ncnt, eps = _peer_a(x1b, _bf(peer_w_query[l]), _bf(peer_subkeys[l, 0]),
                                     _bf(peer_subkeys[l, 1]))
        xt = _peer_b(alpha, x1b, _bf(peer_u[l]), _bf(jnp.transpose(peer_v[l])), rk2, e2, ncnt, eps, x1,
                     _row(ln2_g[l]), _row(ln2_b[l]))
    return xt.reshape(bsz, s, d)
```

```python
import functools

import jax
import jax.numpy as jnp
from jax import lax
from jax.experimental import pallas as pl
from jax.experimental.pallas import tpu as pltpu

F32 = jnp.float32
BF16 = jnp.bfloat16

N_BRANCH = 3
MIX_WIDTH = 512
RG_WIDTH = 512
RG_BLOCK = 64
CONV_WIDTH = 4
RG_C = 8.0
RW_HEADS = 8
RW_HEAD = 64
RW_WIDTH = 512
LORA_W = 64
LORA_A = 64
LORA_V = 32
LORA_G = 128
RW_GN_EPS = 64e-5
RW_COLS = 3 * RW_WIDTH + LORA_W + LORA_A + LORA_G
MLA_HEADS = 8
QK_NOPE = 64
QK_ROPE = 32
V_HEAD = 64
Q_LORA = 256
KV_LORA = 128
ROPE_THETA = 10000.0
MLA_COLS = Q_LORA + KV_LORA + QK_ROPE
N_KEYS = 128
PEER_HEADS = 8
PEER_HALF = 128
PEER_TOPK = 16
LN_EPS = 1e-5

LANES = 128
MIB = 1024 * 1024

MM_TM = 1024
RW_TS = 256
RW_CHUNK = 64
MLA_TM = 512
ATT_TQ = 256
ATT_KB = 512
MERGE_TM = 512
PEERA_TM = 256
PEERB_TM = 512
PEERB_EC = 512


def _cp(sem, vmem_mib):
    return pltpu.CompilerParams(dimension_semantics=sem, vmem_limit_bytes=vmem_mib * MIB)


def _bf(x):
    return x.astype(BF16)


def _dot(a, b):
    return jnp.dot(a, b, preferred_element_type=F32)


def _dot_nt(a, b):
    return lax.dot_general(a, b, (((1,), (1,)), ((), ())), preferred_element_type=F32)


def _split(x):
    hi = x.astype(BF16)
    lo = (x - hi.astype(F32)).astype(BF16)
    return hi, lo


def _dot_hilo(x, w_bf):
    hi, lo = _split(x)
    return _dot(hi, w_bf) + _dot(lo, w_bf)


def _dot3(a, b):
    ah, al = _split(a)
    bh, bl = _split(b)
    return _dot(ah, bh) + (_dot(ah, bl) + _dot(al, bh))


def _shift_rows(x, k, fill, rows):
    return jnp.where(rows >= k, pltpu.roll(x, k, 0), fill)


def _log1p(u):
    w = 1.0 + u
    d = w - 1.0
    return jnp.where(d == 0.0, u, jnp.log(w) * (u / jnp.where(d == 0.0, 1.0, d)))


def _softplus(z):
    return jnp.maximum(z, 0.0) + _log1p(jnp.exp(-jnp.abs(z)))


def _neg_expm1(x):
    u = jnp.exp(x)
    lu = jnp.log(u)
    safe = jnp.where(lu == 0.0, 1.0, lu)
    r = (1.0 - u) * x / safe
    r = jnp.where(u == 1.0, -x, r)
    return jnp.where(u == 0.0, 1.0, r)


def _layer_norm(z, g, b):
    mu = jnp.mean(z, axis=-1, keepdims=True)
    d = z - mu
    var = jnp.mean(d * d, axis=-1, keepdims=True)
    return d * lax.rsqrt(var + LN_EPS) * g + b


def _mm_kernel(x_ref, w_ref, o_ref):
    o_ref[...] = _dot(_bf(x_ref[...]), w_ref[...]).astype(o_ref.dtype)


def _matmul(x, w, tn, name, out_dtype=F32):
    m, k = x.shape
    n = w.shape[1]
    tm = min(MM_TM, m)
    return pl.pallas_call(
        _mm_kernel,
        grid=(m // tm, n // tn),
        in_specs=[pl.BlockSpec((tm, k), lambda i, j: (i, 0)),
                  pl.BlockSpec((k, tn), lambda i, j: (0, j))],
        out_specs=pl.BlockSpec((tm, tn), lambda i, j: (i, j)),
        out_shape=jax.ShapeDtypeStruct((m, n), out_dtype),
        name=name,
        compiler_params=_cp(("parallel", "parallel"), 40),
    )(x, w)


def _rg_kernel(xb_ref, gb_ref, cw_ref, cb_ref, wa_ref, ba_ref, wx_ref, bx_ref, la_ref, o_ref):
    x = xb_ref[...]
    s = x.shape[0]
    rows = lax.broadcasted_iota(jnp.int32, x.shape, 0)
    cw = cw_ref[...]
    xc = cb_ref[...] + cw[0:1, :] * x
    for j in range(1, CONV_WIDTH):
        xc = xc + cw[j:j + 1, :] * _shift_rows(x, j, 0.0, rows)
    xcb = _bf(xc)
    r = jax.nn.sigmoid(_dot(xcb, wa_ref[0]) + ba_ref[...])
    i = jax.nn.sigmoid(_dot(xcb, wx_ref[0]) + bx_ref[...])
    log_a = (-RG_C) * r * _softplus(-la_ref[...])
    a = jnp.exp(log_a)
    mult = jnp.where(rows == 0, 1.0, jnp.sqrt(_neg_expm1(2.0 * log_a)))
    b = mult * (i * xc)
    k = 1
    while k < s:
        a_s = _shift_rows(a, k, 1.0, rows)
        b_s = _shift_rows(b, k, 0.0, rows)
        b = a * b_s + b
        a = a * a_s
        k *= 2
    o_ref[...] = b * jax.nn.gelu(gb_ref[...])


def _rg_branch(p_rg, s, cw, cb, wa_bd, ba, wx_bd, bx, la):
    bsz = p_rg.shape[0] // s
    nct = RG_WIDTH // LANES
    vec = lambda: pl.BlockSpec((1, LANES), lambda b, c: (0, c))
    return pl.pallas_call(
        _rg_kernel,
        grid=(bsz, nct),
        in_specs=[pl.BlockSpec((s, LANES), lambda b, c: (b, c)),
                  pl.BlockSpec((s, LANES), lambda b, c: (b, c + nct)),
                  pl.BlockSpec((CONV_WIDTH, LANES), lambda b, c: (0, c)),
                  vec(),
                  pl.BlockSpec((1, LANES, LANES), lambda b, c: (c, 0, 0)),
                  vec(),
                  pl.BlockSpec((1, LANES, LANES), lambda b, c: (c, 0, 0)),
                  vec(), vec()],
        out_specs=pl.BlockSpec((s, LANES), lambda b, c: (b, c)),
        out_shape=jax.ShapeDtypeStruct((bsz * s, RG_WIDTH), F32),
        name="rg_lru",
        compiler_params=_cp(("parallel", "parallel"), 48),
    )(p_rg, p_rg, cw, cb, wa_bd, ba, wx_bd, bx, la)


def _head_ones(n):
    sh = jnp.int32(RW_HEAD.bit_length() - 1)
    ri = lax.shift_right_logical(lax.broadcasted_iota(jnp.int32, (n, n), 0), sh)
    ci = lax.shift_right_logical(lax.broadcasted_iota(jnp.int32, (n, n), 1), sh)
    return jnp.where(ri == ci, 1.0, 0.0).astype(BF16)


def _rw_prep_kernel(has_vres, pb_ref, prev_ref, mix_ref, w0_ref, w2_ref, a0_ref, a2_ref, g2_ref,
                    kk_ref, ka_ref, *rest):
    if has_vres:
        vf_ref, v0_ref, v1_ref, v2_ref = rest[:4]
        rest = rest[4:]
    r_o, lw_o, k_o, v_o, aa_o, bb_o, g_o = rest
    t = pl.program_id(1)
    p = pb_ref[...]
    rows = lax.broadcasted_iota(jnp.int32, p.shape, 0)
    prev = jnp.where(t == 0, 0.0, prev_ref[7:8, :])
    shifted = jnp.where(rows == 0, prev, pltpu.roll(p, 1, 0))
    p = p + (shifted - p) * mix_ref[...]
    w = RW_WIDTH
    r = p[:, 0:w]
    k = p[:, w:2 * w]
    v = p[:, 2 * w:3 * w]
    xwa = p[:, 3 * w:3 * w + LORA_W + LORA_A]
    xg = p[:, 3 * w + LORA_W + LORA_A:]
    wl = -_softplus(-(w0_ref[...] + _dot(_bf(jnp.tanh(xwa)), w2_ref[...]))) - 0.5
    a = jax.nn.sigmoid(a0_ref[...] + _dot(_bf(xwa), a2_ref[...]))
    g = _dot(_bf(jax.nn.sigmoid(xg)), g2_ref[...])
    if has_vres:
        lo = _dot(_bf(v), v1_ref[...])
        v = v + (vf_ref[...] - v) * jax.nn.sigmoid(v0_ref[...] + _dot(_bf(lo), v2_ref[...]))
    kk = k * kk_ref[...]
    ss = _dot_hilo(kk * kk, _head_ones(w))
    kk = kk / jnp.maximum(jnp.sqrt(ss), 1e-12)
    r_o[...] = r
    lw_o[...] = -jnp.exp(wl)
    k_o[...] = k * (1.0 + (a - 1.0) * ka_ref[...])
    v_o[...] = v
    aa_o[...] = -kk
    bb_o[...] = kk * a
    g_o[...] = g


def _rw_prep(pb, s, mix, w0, w2p, a0, a2p, g2, k_k, k_a, vres):
    tt, cols = pb.shape
    bsz = tt // s
    ts = min(RW_TS, s)
    nts = s // ts
    w = RW_WIDTH
    full = lambda shape: pl.BlockSpec(shape, lambda b, t: (0,) * len(shape))
    tok = lambda c: pl.BlockSpec((ts, c), lambda b, t: (b * nts + t, 0))
    in_specs = [tok(cols),
                pl.BlockSpec((8, cols), lambda b, t: (jnp.maximum((b * nts + t) * (ts // 8) - 1, 0), 0)),
                full((1, cols)), full((1, w)), full((LANES, w)), full((1, w)), full((LANES, w)),
                full((LORA_G, w)), full((1, w)), full((1, w))]
    args = [pb, pb, mix, w0, w2p, a0, a2p, g2, k_k, k_a]
    if vres is not None:
        v_first, v0, v1p, v2p = vres
        in_specs += [tok(w), full((1, w)), full((w, LANES)), full((LANES, w))]
        args += [v_first, v0, v1p, v2p]
    out = jax.ShapeDtypeStruct((tt, w), F32)
    return pl.pallas_call(
        functools.partial(_rw_prep_kernel, vres is not None),
        grid=(bsz, nts),
        in_specs=in_specs,
        out_specs=[tok(w)] * 7,
        out_shape=[out] * 7,
        name="rwkv_prep",
        compiler_params=_cp(("parallel", "parallel"), 40),
    )(*args)


def _rw_scan_kernel(r_ref, lw_ref, k_ref, v_ref, aa_ref, bb_ref, g_ref, rk_ref, gng_ref, gnb_ref,
                    o_ref, st_ref):
    c = pl.program_id(1)

    @pl.when(c == 0)
    def _():
        st_ref[...] = jnp.zeros_like(st_ref)

    ch = r_ref.shape[0]
    c2 = 2 * ch
    rows = lax.broadcasted_iota(jnp.int32, (ch, LANES), 0)
    head0 = lax.broadcasted_iota(jnp.int32, (ch, LANES), 1) < RW_HEAD
    ri = lax.broadcasted_iota(jnp.int32, (c2, c2), 0)
    ci = lax.broadcasted_iota(jnp.int32, (c2, c2), 1)
    blk_lo = jnp.where(ri >= ch, ch, 0)
    in_blk = jnp.where(ci >= blk_lo, jnp.where(ci < blk_lo + ch, 1.0, 0.0), 0.0)
    strict = jnp.where(ci < ri, in_blk, 0.0)
    incl = jnp.where(ci <= ri, in_blk, 0.0)
    eye2 = jnp.where(ci == ri, 1.0, 0.0)
    li = lax.broadcasted_iota(jnp.int32, (LANES, LANES), 0)
    lj = lax.broadcasted_iota(jnp.int32, (LANES, LANES), 1)
    eye_l = li == lj
    ones_hd = _head_ones(LANES)

    def stack(x):
        return jnp.concatenate([jnp.where(head0, x, 0.0), jnp.where(head0, 0.0, x)], axis=0)

    pairs = range(RW_WIDTH // LANES)
    sls = [slice(p * LANES, (p + 1) * LANES) for p in pairs]
    pre = []
    for sl in sls:
        lw = lw_ref[:, sl]
        cum = lw
        kk = 1
        while kk < ch:
            cum = cum + _shift_rows(cum, kk, 0.0, rows)
            kk *= 2
        tot = cum[ch - 1:ch, :]
        e_in = jnp.exp(cum)
        e_out = jnp.exp(-cum)
        e_ex = jnp.exp(cum - lw)
        e_end = jnp.exp(tot - cum)
        r = r_ref[:, sl]
        k = k_ref[:, sl]
        v = v_ref[:, sl]
        aa = aa_ref[:, sl]
        bb = bb_ref[:, sl]
        pre.append(dict(
            r=r, k=k, v=v, tot=tot,
            a_s=stack(aa * e_ex), r_s=stack(r * e_in), b_s=stack(bb * e_out), k_s=stack(k * e_out),
            v_sb=_bf(stack(v)), bh_t=_bf(stack(bb * e_end).T), kh_t=_bf(stack(k * e_end).T)))
    grams = [_dot_nt(_bf(jnp.concatenate([q["a_s"], q["r_s"]], axis=0)),
                     _bf(jnp.concatenate([q["b_s"], q["k_s"]], axis=0))) for q in pre]
    l_ab = [g[:c2, :c2] * strict for g in grams]
    l_ak = [_bf(g[:c2, c2:] * strict) for g in grams]
    m_rb = [_bf(g[c2:, :c2] * incl) for g in grams]
    m_rk = [_bf(g[c2:, c2:] * incl) for g in grams]
    pw = l_ab
    tinv = [eye2 + x for x in l_ab]
    n = 2
    while n < ch:
        pwb = [_bf(x) for x in pw]
        pw = [_dot(x, x) for x in pwb]
        tinv = [t + _dot(_bf(t), _bf(x)) for t, x in zip(tinv, pw)]
        n *= 2
    lakv = [_dot(x, q["v_sb"]) for x, q in zip(l_ak, pre)]
    wb = [_bf(_dot(_bf(t), _bf(jnp.concatenate([q["a_s"], x], axis=1))))
          for t, q, x in zip(tinv, pre, lakv)]
    qy = [_dot(m, w) for m, w in zip(m_rb, wb)]
    mv = [_dot(m, q["v_sb"]) for m, q in zip(m_rk, pre)]
    ac = [_dot(q["bh_t"], w) for q, w in zip(pre, wb)]
    kv = [_dot(q["kh_t"], q["v_sb"]) for q in pre]
    ys = []
    for p in pairs:
        q_hat = pre[p]["r_s"] + qy[p][:, :LANES]
        y_loc = qy[p][:, LANES:] + mv[p]
        a_hat = jnp.where(eye_l, jnp.exp(pre[p]["tot"]), 0.0) + ac[p][:, :LANES]
        c_hat = ac[p][:, LANES:] + kv[p]
        st = st_ref[p]
        y_s = _dot3(q_hat, st) + y_loc
        st_ref[p] = _dot3(a_hat, st) + c_hat
        ys.append(y_s[:ch] + y_s[ch:])
    mus = [_dot_hilo(y, ones_hd) * (1.0 / RW_HEAD) for y in ys]
    ds = [y - m for y, m in zip(ys, mus)]
    vrs = [_dot_hilo(d * d, ones_hd) * (1.0 / RW_HEAD) for d in ds]
    bns = [_dot_hilo(q["r"] * q["k"] * rk_ref[:, sl], ones_hd) for q, sl in zip(pre, sls)]
    for p, sl in zip(pairs, sls):
        yn = ds[p] * lax.rsqrt(vrs[p] + RW_GN_EPS) * gng_ref[:, sl] + gnb_ref[:, sl]
        o_ref[:, sl] = (yn + bns[p] * pre[p]["v"]) * g_ref[:, sl]


def _rw_scan(r, lw, k, v, aa, bb, g, s, r_k, gn_g, gn_b):
    tt, w = r.shape
    bsz = tt // s
    ch = min(RW_CHUNK, s)
    nc = s // ch
    tok = pl.BlockSpec((ch, w), lambda b, c: (b * nc + c, 0))
    vec = pl.BlockSpec((1, w), lambda b, c: (0, 0))
    return pl.pallas_call(
        _rw_scan_kernel,
        grid=(bsz, nc),
        in_specs=[tok] * 7 + [vec] * 3,
        out_specs=tok,
        out_shape=jax.ShapeDtypeStruct((tt, w), F32),
        scratch_shapes=[pltpu.VMEM((w // LANES, LANES, LANES), F32)],
        name="rwkv_scan",
        compiler_params=_cp(("parallel", "arbitrary"), 40),
    )(r, lw, k, v, aa, bb, g, r_k, gn_g, gn_b)


def _rms(x, g):
    return x * lax.rsqrt(jnp.mean(x * x, axis=-1, keepdims=True) + 1e-6) * g


def _mla_proj_kernel(pm_ref, qn_ref, wuq_ref, kvn_ref, wukv_ref, cos_ref, sin_ref, q_o, k_o, v_o):
    c = pm_ref[...]
    hn = MLA_HEADS * QK_NOPE
    q = _dot(_bf(_rms(c[:, :Q_LORA], qn_ref[...])), wuq_ref[...])
    kv = _dot(_bf(_rms(c[:, Q_LORA:Q_LORA + KV_LORA], kvn_ref[...])), wukv_ref[...])
    k1 = c[:, Q_LORA + KV_LORA:Q_LORA + KV_LORA + LANES]
    k2 = c[:, Q_LORA + KV_LORA + LANES:]
    cos = cos_ref[...]
    sin = sin_ref[...]
    q1 = q[:, hn:hn + LANES]
    q2 = q[:, hn + LANES:]
    q_o[:, :hn] = _bf(q[:, :hn])
    q_o[:, hn:hn + LANES] = _bf(q1 * cos - q2 * sin)
    q_o[:, hn + LANES:] = _bf(q1 * sin + q2 * cos)
    k_o[:, :hn] = _bf(kv[:, :hn])
    k_o[:, hn:hn + LANES] = _bf(k1 * cos - k2 * sin)
    k_o[:, hn + LANES:] = _bf(k1 * sin + k2 * cos)
    v_o[...] = _bf(kv[:, hn:])


def _mla_proj(pm, q_norm, wuq, kv_norm, wukv, cos_t, sin_t, s):
    t, cols = pm.shape
    tm = min(MLA_TM, s)
    nst = s // tm
    hn = MLA_HEADS * QK_NOPE
    qw = hn + 2 * LANES
    full = lambda shape: pl.BlockSpec(shape, lambda i: (0,) * len(shape))
    tok = lambda c: pl.BlockSpec((tm, c), lambda i: (i, 0))
    pos = pl.BlockSpec((tm, LANES), lambda i: (i % nst, 0))
    return pl.pallas_call(
        _mla_proj_kernel,
        grid=(t // tm,),
        in_specs=[tok(cols), full((1, Q_LORA)), full((Q_LORA, qw)), full((1, KV_LORA)),
                  full((KV_LORA, 2 * hn)), pos, pos],
        out_specs=[tok(qw), tok(qw), tok(hn)],
        out_shape=[jax.ShapeDtypeStruct((t, qw), BF16), jax.ShapeDtypeStruct((t, qw), BF16),
                   jax.ShapeDtypeStruct((t, hn), BF16)],
        name="mla_proj",
        compiler_params=_cp(("parallel",), 32),
    )(pm, q_norm, wuq, kv_norm, wukv, cos_t, sin_t)


def _attn_kernel(qn_ref, qp_ref, kn_ref, kp_ref, v_ref, o_ref, *, scale):
    pair = pl.program_id(1)
    i = pl.program_id(2)
    tq = qn_ref.shape[0]
    s_len = kn_ref.shape[0]
    kb = min(ATT_KB, s_len)
    half = QK_ROPE // 2
    lane_n = lax.broadcasted_iota(jnp.int32, (1, LANES), 1)
    lane_p = lax.broadcasted_iota(jnp.int32, (1, 2 * LANES), 1) & (LANES - 1)
    zero = jnp.zeros((), BF16)
    need = ((i + 1) * tq + kb - 1) // kb
    for n in range(1, s_len // kb + 1):
        @pl.when(need == n)
        def _(n=n):
            qn = qn_ref[...]
            qp = qp_ref[...]
            kn = kn_ref[:n * kb, :]
            kp = kp_ref[:n * kb, :]
            v = v_ref[:n * kb, :]
            qpos = i * tq + lax.broadcasted_iota(jnp.int32, (tq, n * kb), 0)
            kpos = lax.broadcasted_iota(jnp.int32, (tq, n * kb), 1)
            causal = kpos <= qpos
            out = None
            for hh in range(2):
                in_head = (lane_n < QK_NOPE) if hh == 0 else (lane_n >= QK_NOPE)
                lo = (2 * pair + hh) * half
                qp_h = jnp.where(lane_p >= lo, jnp.where(lane_p < lo + half, qp, zero), zero)
                s = (_dot_nt(jnp.where(in_head, qn, zero), kn) + _dot_nt(qp_h, kp)) * scale
                s = jnp.where(causal, s, -1e30)
                m = jnp.max(s, axis=-1, keepdims=True)
                p = jnp.exp(s - m)
                l = jnp.sum(p, axis=-1, keepdims=True)
                o = _dot(_bf(p), v) / l
                out = o if out is None else jnp.where(in_head, o, out)
            o_ref[...] = out.astype(o_ref.dtype)


def _attention(q_all, k_all, v_all, s):
    t = q_all.shape[0]
    bsz = t // s
    hn = MLA_HEADS * QK_NOPE
    tq = min(ATT_TQ, s)
    nq = s // tq
    npair = hn // LANES
    rope_blk = hn // (2 * LANES)
    return pl.pallas_call(
        functools.partial(_attn_kernel, scale=(QK_NOPE + QK_ROPE) ** -0.5),
        grid=(bsz, npair, nq),
        in_specs=[pl.BlockSpec((tq, LANES), lambda b, p, i: (b * nq + i, p)),
                  pl.BlockSpec((tq, 2 * LANES), lambda b, p, i: (b * nq + i, rope_blk)),
                  pl.BlockSpec((s, LANES), lambda b, p, i: (b, p)),
                  pl.BlockSpec((s, 2 * LANES), lambda b, p, i: (b, rope_blk)),
                  pl.BlockSpec((s, LANES), lambda b, p, i: (b, p))],
        out_specs=pl.BlockSpec((tq, LANES), lambda b, p, i: (b * nq + i, p)),
        out_shape=jax.ShapeDtypeStruct((t, hn), BF16),
        name="mla_attention",
        compiler_params=_cp(("parallel", "parallel", "parallel"), 40),
    )(q_all, q_all, k_all, k_all, v_all)


def _merge_kernel(alpha, ya_ref, yb_ref, yc_ref, pg_ref, x_ref, wb_ref, wo_ref, g_ref, b_ref,
                  o_ref, ob_ref):
    d = x_ref.shape[1]
    merged = jax.nn.sigmoid(pg_ref[:, 0:d]) * _dot(_bf(ya_ref[...]), wb_ref[0])
    merged = merged + jax.nn.sigmoid(pg_ref[:, d:2 * d]) * _dot(_bf(yb_ref[...]), wb_ref[1])
    merged = merged + jax.nn.sigmoid(pg_ref[:, 2 * d:3 * d]) * _dot(_bf(yc_ref[...]), wb_ref[2])
    z = alpha * x_ref[...] + _dot(_bf(merged), wo_ref[...])
    out = _layer_norm(z, g_ref[...], b_ref[...])
    o_ref[...] = out
    ob_ref[...] = _bf(out)


def _merge(alpha, ya, yb, yc, pg, x, w_branch, w_out, g, b):
    t, d = x.shape
    tm = min(MERGE_TM, t)
    tok = lambda c: pl.BlockSpec((tm, c), lambda i: (i, 0))
    full = lambda shape: pl.BlockSpec(shape, lambda i: (0,) * len(shape))
    return pl.pallas_call(
        functools.partial(_merge_kernel, alpha),
        grid=(t // tm,),
        in_specs=[tok(MIX_WIDTH), tok(MIX_WIDTH), tok(MIX_WIDTH), tok(N_BRANCH * d), tok(d),
                  full((N_BRANCH, MIX_WIDTH, d)), full((d, d)), full((1, d)), full((1, d))],
        out_specs=[tok(d), tok(d)],
        out_shape=[jax.ShapeDtypeStruct((t, d), F32), jax.ShapeDtypeStruct((t, d), BF16)],
        name="merge_ln",
        compiler_params=_cp(("parallel",), 48),
    )(ya, yb, yc, pg, x, w_branch, w_out, g, b)


def _top16(x, kidx):
    rank = jnp.full(x.shape, float(PEER_TOPK), F32)
    vals = []
    nk = float(x.shape[0])
    for b in range(PEER_TOPK):
        m = jnp.max(x, axis=0, keepdims=True)
        first = jnp.min(jnp.where(x == m, kidx, nk), axis=0, keepdims=True)
        hit = kidx == first
        rank = jnp.where(hit, float(b), rank)
        x = jnp.where(hit, -jnp.inf, x)
        vals.append(m)
    return jnp.concatenate(vals, axis=0), rank


def _peer_a_kernel(x_ref, wq_ref, k1_ref, k2_ref, rk2_o, e2_o, n_o, eps_o, q_scr):
    q_scr[...] = _bf(_dot(x_ref[...], wq_ref[...]))
    tm = x_ref.shape[0]
    kidx = lax.broadcasted_iota(jnp.int32, (N_KEYS, tm), 0).astype(F32)
    aidx = lax.broadcasted_iota(jnp.int32, (PEER_TOPK, tm), 0).astype(F32)

    def head(h, carry):
        base = pl.multiple_of(h * (2 * PEER_HALF), 2 * PEER_HALF)
        s1 = _dot_nt(k1_ref[...], q_scr[:, pl.ds(base, PEER_HALF)])
        s2 = _dot_nt(k2_ref[...], q_scr[:, pl.ds(base + PEER_HALF, PEER_HALF)])
        v1, rk1 = _top16(s1, kidx)
        v2, rk2 = _top16(s2, kidx)
        cnt = jnp.zeros((PEER_TOPK, tm), F32)
        front = v1 + v2[0:1, :]
        for _ in range(PEER_TOPK):
            m = jnp.max(front, axis=0, keepdims=True)
            a_star = jnp.min(jnp.where(front == m, aidx, float(PEER_TOPK)), axis=0, keepdims=True)
            hit = aidx == a_star
            cnt = cnt + jnp.where(hit, 1.0, 0.0)
            nxt = jnp.sum(jnp.where(hit, cnt, 0.0), axis=0, keepdims=True)
            v2n = jnp.max(jnp.where(aidx == nxt, v2, -jnp.inf), axis=0, keepdims=True)
            front = jnp.where(hit, v1 + v2n, front)
        e1 = jnp.exp(v1 - v1[0:1, :])
        e2 = jnp.exp(v2 - v2[0:1, :])
        pref = jnp.zeros((PEER_TOPK, tm), F32)
        for b in range(PEER_TOPK):
            pref = pref + jnp.where(cnt > float(b), e2[b:b + 1, :], 0.0)
        z = jnp.sum(e1 * pref, axis=0, keepdims=True)
        ncount = jnp.zeros((N_KEYS, tm), F32)
        for a in range(PEER_TOPK):
            ncount = jnp.where(rk1 == float(a), cnt[a:a + 1, :], ncount)
        rk2_o[h] = _bf(rk2)
        e2_o[h] = _bf(jnp.exp(s2 - v2[0:1, :]))
        n_o[h] = ncount
        eps_o[h] = jnp.exp(s1 - v1[0:1, :]) / z
        return carry

    lax.fori_loop(0, PEER_HEADS, head, 0)


def _peer_a(xb, wq, k1, k2):
    t, d = xb.shape
    tm = min(PEERA_TM, t)
    qd = wq.shape[1]
    full = lambda shape: pl.BlockSpec(shape, lambda i: (0,) * len(shape))
    tab = pl.BlockSpec((PEER_HEADS, N_KEYS, tm), lambda i: (0, 0, i))
    out = jax.ShapeDtypeStruct((PEER_HEADS, N_KEYS, t), F32)
    out_bf = jax.ShapeDtypeStruct((PEER_HEADS, N_KEYS, t), BF16)
    return pl.pallas_call(
        _peer_a_kernel,
        grid=(t // tm,),
        in_specs=[pl.BlockSpec((tm, d), lambda i: (i, 0)), full((d, qd)),
                  full((N_KEYS, PEER_HALF)), full((N_KEYS, PEER_HALF))],
        out_specs=[tab] * 4,
        out_shape=[out_bf, out_bf, out, out],
        scratch_shapes=[pltpu.VMEM((tm, qd), BF16)],
        name="peer_topk",
        compiler_params=_cp(("parallel",), 40),
    )(xb, wq, k1, k2)


def _peer_b_kernel(alpha, xb_ref, u0_ref, uo_ref, un_ref, vtp_ref, vte_ref, vtl_ref,
                   rk2_ref, e2_ref, n_ref, eps_ref, x_ref, g_ref, b_ref,
                   o_ref, acc_ref, acta_ref, actb_ref, ha_ref, hb_ref):
    g = pl.program_id(1)
    nblk = uo_ref.shape[0] // N_KEYS
    xb = xb_ref[...]

    @pl.when(g == 0)
    def _():
        acc_ref[...] = jnp.zeros_like(acc_ref)
        hb_ref[...] = jnp.zeros_like(hb_ref)
        acta_ref[...] = _dot_nt(u0_ref[...], xb)

    def phase(half, act_ref, h_out_ref, vt_ref, h_in_ref, u_ref, act_next_ref):
        acc_ref[...] += _dot(vt_ref[...], h_in_ref[...])
        act_next_ref[...] = _dot_nt(u_ref[...], xb)
        for il in range(nblk):
            i = half * nblk + il
            rows = slice(il * N_KEYS, (il + 1) * N_KEYS)
            gate = None
            for h in range(PEER_HEADS):
                cnt = _bf(n_ref[h, i:i + 1, :])
                eps = _bf(eps_ref[h, i:i + 1, :])
                term = jnp.where(rk2_ref[h] < cnt, e2_ref[h] * eps, jnp.zeros((), BF16))
                gate = term if gate is None else gate + term
            h_out_ref[rows, :] = _bf(jax.nn.gelu(act_ref[rows, :])) * gate

    phase(0, acta_ref, ha_ref, vtp_ref, hb_ref, uo_ref, actb_ref)
    phase(1, actb_ref, hb_ref, vte_ref, ha_ref, un_ref, acta_ref)

    @pl.when(g == pl.num_programs(1) - 1)
    def _():
        acc = acc_ref[...] + _dot(vtl_ref[...], hb_ref[...])
        z = alpha * x_ref[...] + acc.T
        o_ref[...] = _layer_norm(z, g_ref[...], b_ref[...])


def _peer_b(alpha, xb, u_bf, vt_bf, rk2, e2, ncnt, eps, x, g, b):
    t, d = x.shape
    ne = u_bf.shape[0]
    tm = min(PEERB_TM, t)
    ec = min(PEERB_EC, ne)
    nch = ne // ec
    assert nch % 2 == 0
    tab = pl.BlockSpec((PEER_HEADS, N_KEYS, tm), lambda i, j: (0, 0, i))
    rowtab = pl.BlockSpec((PEER_HEADS, 2 * ec // N_KEYS, tm), lambda i, j: (0, j, i))
    vec = pl.BlockSpec((1, d), lambda i, j: (0, 0))
    ublk = lambda f: pl.BlockSpec((ec, d), lambda i, j: (f(j), 0))
    vblk = lambda f: pl.BlockSpec((d, ec), lambda i, j: (0, f(j)))
    return pl.pallas_call(
        functools.partial(_peer_b_kernel, alpha),
        grid=(t // tm, nch // 2),
        in_specs=[pl.BlockSpec((tm, d), lambda i, j: (i, 0)),
                  ublk(lambda j: 0), ublk(lambda j: 2 * j + 1),
                  ublk(lambda j: jnp.minimum(2 * j + 2, nch - 1)),
                  vblk(lambda j: jnp.maximum(2 * j - 1, 0)), vblk(lambda j: 2 * j),
                  vblk(lambda j: nch - 1),
                  tab, tab, rowtab, rowtab,
                  pl.BlockSpec((tm, d), lambda i, j: (i, 0)), vec, vec],
        out_specs=pl.BlockSpec((tm, d), lambda i, j: (i, 0)),
        out_shape=jax.ShapeDtypeStruct((t, d), F32),
        scratch_shapes=[pltpu.VMEM((d, tm), F32), pltpu.VMEM((ec, tm), F32), pltpu.VMEM((ec, tm), F32),
                        pltpu.VMEM((ec, tm), BF16), pltpu.VMEM((ec, tm), BF16)],
        name="peer_dense",
        compiler_params=_cp(("parallel", "arbitrary"), 52),
    )(xb, u_bf, u_bf, u_bf, vt_bf, vt_bf, vt_bf, rk2, e2, ncnt, eps, x, g, b)


def _pair_block_diag(w):
    nb = w.shape[0]
    w4 = w.reshape(nb // 2, 2, RG_BLOCK, RG_BLOCK)
    z = jnp.zeros_like(w4[:, 0])
    top = jnp.concatenate([w4[:, 0], z], axis=-1)
    bot = jnp.concatenate([z, w4[:, 1]], axis=-1)
    return _bf(jnp.concatenate([top, bot], axis=-2))


def _row(v):
    return v.reshape(1, -1)


def kernel(x, w_in, rg_conv_w, rg_conv_b, rg_wa, rg_ba, rg_wx, rg_bx, rg_log_a, rw_mix, rw_w0, rw_w2, rw_a0, rw_a2, rw_g2, rw_v0, rw_v1, rw_v2, rw_k_k, rw_k_a, rw_r_k, rw_gn_g, rw_gn_b, mla_q_norm, mla_w_uq, mla_kv_norm, mla_w_ukv, w_branch, w_out, ln1_g, ln1_b, peer_w_query, peer_subkeys, peer_u, peer_v, ln2_g, ln2_b):
    bsz, s, d = x.shape
    depth = w_in.shape[0]
    t = bsz * s
    alpha = (2.0 * depth) ** 0.25
    hn = MLA_HEADS * QK_NOPE
    half = QK_ROPE // 2

    pos = jnp.arange(s, dtype=F32)
    inv_freq = ROPE_THETA ** (-jnp.arange(half, dtype=F32) / half)
    ang = pos[:, None] * inv_freq[None, :]
    cos_t = jnp.tile(jnp.cos(ang).astype(F32), (1, MLA_HEADS))
    sin_t = jnp.tile(jnp.sin(ang).astype(F32), (1, MLA_HEADS))

    xt = x.reshape(t, d)
    v_first = None
    for l in range(depth):
        wl = w_in[l]
        o_rw = 2 * RG_WIDTH
        o_mla = o_rw + RW_COLS
        o_gate = o_mla + MLA_COLS
        o_rope = o_mla + Q_LORA + KV_LORA
        w_rg = _bf(wl[:, :o_rw])
        w_rw = _bf(wl[:, o_rw:o_mla])
        w_mla = _bf(jnp.concatenate(
            [wl[:, o_mla:o_rope],
             jnp.tile(wl[:, o_rope:o_rope + half], (1, MLA_HEADS)),
             jnp.tile(wl[:, o_rope + half:o_gate], (1, MLA_HEADS))], axis=1))
        w_gate = _bf(wl[:, o_gate:])

        p_rg = _matmul(xt, w_rg, 512, "proj_rg")
        p_rw = _matmul(xt, w_rw, RW_COLS // 2, "proj_rw")
        p_mla = _matmul(xt, w_mla, w_mla.shape[1], "proj_mla")
        p_gate = _matmul(xt, w_gate, 1024, "proj_gate")

        y_a = _rg_branch(p_rg, s, rg_conv_w[l], _row(rg_conv_b[l]),
                         _pair_block_diag(rg_wa[l]), _row(rg_ba[l]),
                         _pair_block_diag(rg_wx[l]), _row(rg_bx[l]), _row(rg_log_a[l]))

        zl = jnp.zeros((LORA_W, RW_WIDTH), F32)
        w2p = _bf(jnp.concatenate([rw_w2[l], zl], axis=0))
        a2p = _bf(jnp.concatenate([zl, rw_a2[l]], axis=0))
        vres = None
        if l > 0:
            v1p = _bf(jnp.pad(rw_v1[l - 1], ((0, 0), (0, LANES - LORA_V))))
            v2p = _bf(jnp.pad(rw_v2[l - 1], ((0, LANES - LORA_V), (0, 0))))
            vres = (v_first, _row(rw_v0[l - 1]), v1p, v2p)
        r_, lw_, k_, v_, aa_, bb_, g_ = _rw_prep(
            p_rw, s, _row(rw_mix[l]), _row(rw_w0[l]), w2p, _row(rw_a0[l]), a2p,
            _bf(rw_g2[l]), _row(rw_k_k[l]), _row(rw_k_a[l]), vres)
        if l == 0:
            v_first = v_
        y_b = _rw_scan(r_, lw_, k_, v_, aa_, bb_, g_, s, _row(rw_r_k[l]), _row(rw_gn_g[l]), _row(rw_gn_b[l]))

        wuq = mla_w_uq[l].reshape(Q_LORA, MLA_HEADS, QK_NOPE + QK_ROPE)
        wuq = _bf(jnp.concatenate([wuq[:, :, :QK_NOPE].reshape(Q_LORA, hn),
                                   wuq[:, :, QK_NOPE:QK_NOPE + half].reshape(Q_LORA, LANES),
                                   wuq[:, :, QK_NOPE + half:].reshape(Q_LORA, LANES)], axis=1))
        wukv = mla_w_ukv[l].reshape(KV_LORA, MLA_HEADS, QK_NOPE + V_HEAD)
        wukv = _bf(jnp.concatenate([wukv[:, :, :QK_NOPE].reshape(KV_LORA, hn),
                                    wukv[:, :, QK_NOPE:].reshape(KV_LORA, hn)], axis=1))
        q_all, k_all, v_all = _mla_proj(p_mla, _row(mla_q_norm[l]), wuq, _row(mla_kv_norm[l]), wukv,
                                        cos_t, sin_t, s)
        y_c = _attention(q_all, k_all, v_all, s)

        x1, x1b = _merge(alpha, y_a, y_b, y_c, p_gate, xt,
                         _bf(w_branch[l]), _bf(w_out[l]), _row(ln1_g[l]), _row(ln1_b[l]))

        rk2, e2, ncnt, eps = _peer_a(x1b, _bf(peer_w_query[l]), _bf(peer_subkeys[l, 0]),
                                     _bf(peer_subkeys[l, 1]))
        xt = _peer_b(alpha, x1b, _bf(peer_u[l]), _bf(jnp.transpose(peer_v[l])), rk2, e2, ncnt, eps, x1,
                     _row(ln2_g[l]), _row(ln2_b[l]))
    return xt.reshape(bsz, s, d)
```
